```python
import math
import jax, jax.numpy as jnp
from jax import lax
import numpy as np

D_MODEL = 2048
BATCH = 32
SEQ = 256
DEPTH = 2
DEC_BATCH = 2
DEC_SEQ = 1024
PAST_LEN = 256

GRID_W = 64
Q_BLOCK = 128
ROPE_THETA = 10000.0
EPS = 1e-6
N_EVEN = (DEPTH + 1) // 2
N_ODD = DEPTH // 2
H_A = 8
DK_A = 64
DV_A = 2 * DK_A
W_A = H_A * DV_A
POOL_WINDOWS = (2, 4, 8, 16)
N_POOL = len(POOL_WINDOWS)
W_B = D_MODEL - W_A
GW_B = W_B // N_POOL
EVEN_IN = 3 * W_A + W_B + (W_A + W_B)
HD_C = 128
H_C = D_MODEL // HD_C
KVH_C = 4
G_C = H_C // KVH_C
W_C = H_C * HD_C
ODD_IN = W_C + 2 * KVH_C * HD_C + W_C

kernel_name = "hybrid_diffattn_pool_gqa_prefix_dit_step"


def rmsnorm(x, w):
    xf = x.astype(jnp.float32)
    y = xf * lax.rsqrt(jnp.mean(xf * xf, axis=-1, keepdims=True) + EPS) * w.astype(jnp.float32)
    return y.astype(x.dtype)


def axial_rope_tables(n_tokens, dim):
    rows = n_tokens // GRID_W
    row = jnp.repeat(jnp.arange(rows), GRID_W).astype(jnp.float32)
    col = jnp.tile(jnp.arange(GRID_W), rows).astype(jnp.float32)
    quarter = dim // 4
    freqs = ROPE_THETA ** (-jnp.arange(quarter, dtype=jnp.float32) / quarter)
    ar = row[:, None] * freqs
    ac = col[:, None] * freqs
    cos = jnp.concatenate([jnp.cos(ar), jnp.cos(ar), jnp.cos(ac), jnp.cos(ac)], axis=-1)
    sin = jnp.concatenate([jnp.sin(ar), jnp.sin(ar), jnp.sin(ac), jnp.sin(ac)], axis=-1)
    return cos, sin


def apply_rope(x, cos, sin):
    xf = x.astype(jnp.float32)
    x0, x1, x2, x3 = jnp.split(xf, 4, axis=-1)
    rot = jnp.concatenate([-x1, x0, -x3, x2], axis=-1)
    shape = (cos.shape[0],) + (1,) * (x.ndim - 3) + (cos.shape[-1],)
    return (xf * cos.reshape(shape) + rot * sin.reshape(shape)).astype(x.dtype)


def sweep_query_blocks(fn, q):
    lead, L, d = q.shape[:-2], q.shape[-2], q.shape[-1]
    nb = L // Q_BLOCK
    qb = jnp.moveaxis(q.reshape(*lead, nb, Q_BLOCK, d), -3, 0)
    ob = lax.map(fn, qb)
    ob = jnp.moveaxis(ob, 0, -3)
    return ob.reshape(*ob.shape[:-3], L, ob.shape[-1])


def modulate(x, cond, norm_w, ada_w, ada_b):
    m = jax.nn.silu(cond) @ ada_w + ada_b
    shift, scale, gate = jnp.split(m, 3, axis=-1)
    h = rmsnorm(x, norm_w) * (1 + scale[:, None]) + shift[:, None]
    return h, gate[:, None]


def diff_attention(q, k, v, lam, lam_init, subln_w):
    k1, k2 = k[..., :DK_A], k[..., DK_A:]
    vf = v.astype(jnp.float32)
    scale = DK_A ** -0.5

    def block(qb):
        s1 = jnp.einsum('bhqd,bhkd->bhqk', qb[:, :, 0], k1).astype(jnp.float32) * scale
        s2 = jnp.einsum('bhqd,bhkd->bhqk', qb[:, :, 1], k2).astype(jnp.float32) * scale
        p = jax.nn.softmax(s1, axis=-1) - lam * jax.nn.softmax(s2, axis=-1)
        return jnp.einsum('bhqk,bhkd->bhqd', p, vf)

    o = sweep_query_blocks(block, q)
    o = rmsnorm(o, subln_w) * (1.0 - lam_init)
    return o.astype(v.dtype)


def multiscale_pool(u, pool_w, pool_scale):
    B, L, _ = u.shape
    ug = u.reshape(B, L, N_POOL, GW_B).astype(jnp.float32)
    S = jnp.concatenate([jnp.zeros((B, 1, N_POOL, GW_B), jnp.float32),
                         jnp.cumsum(ug, axis=1)], axis=1)
    t = np.arange(L)[:, None]
    half = np.array(POOL_WINDOWS)[None, :] // 2
    lo = np.clip(t - half, 0, L)
    hi = np.clip(t + half, 0, L)
    gidx = np.arange(N_POOL)[None, :]
    cnt = jnp.asarray((hi - lo).astype(np.float32))[None, :, :, None]
    mean = (S[:, hi, gidx] - S[:, lo, gidx]) / cnt
    out = jnp.einsum('blgc,gcd->blgd', mean - ug, pool_w.astype(jnp.float32))
    return (out.reshape(B, L, W_B) * pool_scale).astype(u.dtype)


def even_mixer(h, p, lam, lam_init, rope, ctx_k, ctx_v):
    B, L, _ = h.shape
    proj = h @ p['w_in']
    q, k, v, u, g = jnp.split(proj, [W_A, 2 * W_A, 3 * W_A, 3 * W_A + W_B], axis=-1)
    q = rmsnorm(q.reshape(B, L, H_A, 2, DK_A), p['q_norm_w'])
    k = rmsnorm(k.reshape(B, L, H_A, 2, DK_A), p['k_norm_w'])
    if rope is not None:
        q = apply_rope(q, *rope)
        k = apply_rope(k, *rope)
    q = q.transpose(0, 2, 3, 1, 4)
    k_own = k.transpose(0, 2, 1, 3, 4).reshape(B, H_A, L, 2 * DK_A)
    v_own = v.reshape(B, L, H_A, DV_A).transpose(0, 2, 1, 3)
    if ctx_k is None:
        k_all, v_all = k_own, v_own
    else:
        k_all = jnp.concatenate([ctx_k.astype(k_own.dtype), k_own], axis=2)
        v_all = jnp.concatenate([ctx_v.astype(v_own.dtype), v_own], axis=2)
    attn = diff_attention(q, k_all, v_all, lam, lam_init, p['subln_w'])
    attn = attn.transpose(0, 2, 1, 3).reshape(B, L, W_A)
    pool = multiscale_pool(u, p['pool_w'], p['pool_scale'])
    y = jnp.concatenate([attn, pool], axis=-1) * jax.nn.silu(g)
    return y @ p['w_out'], k_own, v_own


def gqa_attention(q, k, v):
    vf = v.astype(jnp.float32)
    scale = HD_C ** -0.5

    def block(qb):
        s = jnp.einsum('bgrqd,bgkd->bgrqk', qb, k).astype(jnp.float32) * scale
        return jnp.einsum('bgrqk,bgkd->bgrqd', jax.nn.softmax(s, axis=-1), vf)

    return sweep_query_blocks(block, q).astype(v.dtype)


def odd_mixer(h, p, rope, ctx_k, ctx_v):
    B, L, _ = h.shape
    proj = h @ p['w_in']
    q, k, v, g = jnp.split(proj, [W_C, W_C + KVH_C * HD_C, W_C + 2 * KVH_C * HD_C], axis=-1)
    q = rmsnorm(q.reshape(B, L, H_C, HD_C), p['q_norm_w'])
    k = rmsnorm(k.reshape(B, L, KVH_C, HD_C), p['k_norm_w'])
    if rope is not None:
        q = apply_rope(q, *rope)
        k = apply_rope(k, *rope)
    q = q.reshape(B, L, KVH_C, G_C, HD_C).transpose(0, 2, 3, 1, 4)
    k_own = k.transpose(0, 2, 1, 3)
    v_own = v.reshape(B, L, KVH_C, HD_C).transpose(0, 2, 1, 3)
    if ctx_k is None:
        k_all, v_all = k_own, v_own
    else:
        k_all = jnp.concatenate([ctx_k.astype(k_own.dtype), k_own], axis=2)
        v_all = jnp.concatenate([ctx_v.astype(v_own.dtype), v_own], axis=2)
    o = gqa_attention(q, k_all, v_all)
    o = o.transpose(0, 3, 1, 2, 4).reshape(B, L, W_C)
    return (o * jax.nn.silu(g)) @ p['w_out'], k_own, v_own


def setup_inputs(seed: int = 0) -> dict:
    key = jax.random.key(seed)
    ks = jax.random.split(key, 32)
    f32 = jnp.float32
    nrm = lambda k, shape, s=1.0: (jax.random.normal(k, shape, f32) * s).astype(f32)
    D = D_MODEL
    return {
        'x_prompt': nrm(ks[0], (BATCH, SEQ, D)),
        'x_sample': nrm(ks[1], (DEC_BATCH, DEC_SEQ, D)),
        'cache_a_k': nrm(ks[2], (DEC_BATCH, N_EVEN, H_A, PAST_LEN, 2 * DK_A)),
        'cache_a_v': nrm(ks[3], (DEC_BATCH, N_EVEN, H_A, PAST_LEN, DV_A)),
        'cache_c_k': nrm(ks[4], (DEC_BATCH, N_ODD, KVH_C, PAST_LEN, HD_C)),
        'cache_c_v': nrm(ks[5], (DEC_BATCH, N_ODD, KVH_C, PAST_LEN, HD_C)),
        'c': nrm(ks[6], (DEC_BATCH, D)),
        'c_ctx': nrm(ks[7], (D,)),
        'norm_w': 1.0 + nrm(ks[8], (DEPTH, D), 0.05),
        'ada_w': nrm(ks[9], (DEPTH, D, 3 * D), D ** -0.5),
        'ada_b': nrm(ks[10], (DEPTH, 3 * D), 0.02),
        'even_w_in': nrm(ks[11], (N_EVEN, D, EVEN_IN), D ** -0.5),
        'even_q_norm_w': 1.0 + nrm(ks[12], (N_EVEN, DK_A), 0.05),
        'even_k_norm_w': 1.0 + nrm(ks[13], (N_EVEN, DK_A), 0.05),
        'even_lam_q1': nrm(ks[14], (N_EVEN, DK_A), 0.1),
        'even_lam_k1': nrm(ks[15], (N_EVEN, DK_A), 0.1),
        'even_lam_q2': nrm(ks[16], (N_EVEN, DK_A), 0.1),
        'even_lam_k2': nrm(ks[17], (N_EVEN, DK_A), 0.1),
        'even_subln_w': 1.0 + nrm(ks[18], (N_EVEN, DV_A), 0.05),
        'even_pool_w': nrm(ks[19], (N_EVEN, N_POOL, GW_B, GW_B), GW_B ** -0.5),
        'even_pool_scale': 1.0 + nrm(ks[20], (N_EVEN, W_B), 0.1),
        'even_w_out': nrm(ks[21], (N_EVEN, W_A + W_B, D), (W_A + W_B) ** -0.5),
        'gqa_w_in': nrm(ks[22], (N_ODD, D, ODD_IN), D ** -0.5),
        'gqa_q_norm_w': 1.0 + nrm(ks[23], (N_ODD, HD_C), 0.05),
        'gqa_k_norm_w': 1.0 + nrm(ks[24], (N_ODD, HD_C), 0.05),
        'gqa_w_out': nrm(ks[25], (N_ODD, W_C, D), W_C ** -0.5),
    }


def reference(x_prompt, x_sample, cache_a_k, cache_a_v, cache_c_k, cache_c_v, c, c_ctx,
              norm_w, ada_w, ada_b,
              even_w_in, even_q_norm_w, even_k_norm_w, even_lam_q1, even_lam_k1,
              even_lam_q2, even_lam_k2, even_subln_w, even_pool_w, even_pool_scale,
              even_w_out, gqa_w_in, gqa_q_norm_w, gqa_k_norm_w, gqa_w_out):
    n_lat = x_sample.shape[1]
    rope_a = axial_rope_tables(n_lat, DK_A)
    rope_c = axial_rope_tables(n_lat, HD_C)
    cond_ctx = jnp.broadcast_to(c_ctx[None, :], (x_prompt.shape[0], D_MODEL))

    y_p, y_s = x_prompt, x_sample
    new_a_k, new_a_v, new_c_k, new_c_v = [], [], [], []
    for i in range(DEPTH):
        h_p, gate_p = modulate(y_p, cond_ctx, norm_w[i], ada_w[i], ada_b[i])
        h_s, gate_s = modulate(y_s, c, norm_w[i], ada_w[i], ada_b[i])
        if i % 2 == 0:
            j = i // 2
            p = {'w_in': even_w_in[j], 'q_norm_w': even_q_norm_w[j], 'k_norm_w': even_k_norm_w[j],
                 'subln_w': even_subln_w[j], 'pool_w': even_pool_w[j],
                 'pool_scale': even_pool_scale[j], 'w_out': even_w_out[j]}
            lam_init = 0.8 - 0.6 * math.exp(-0.3 * i)
            lam = (jnp.exp(jnp.sum(even_lam_q1[j].astype(jnp.float32) * even_lam_k1[j].astype(jnp.float32)))
                   - jnp.exp(jnp.sum(even_lam_q2[j].astype(jnp.float32) * even_lam_k2[j].astype(jnp.float32)))
                   + lam_init)
            out_p, k_new, v_new = even_mixer(h_p, p, lam, lam_init, None, None, None)
            out_s, _, _ = even_mixer(h_s, p, lam, lam_init, rope_a, cache_a_k[:, j], cache_a_v[:, j])
            new_a_k.append(k_new)
            new_a_v.append(v_new)
        else:
            j = i // 2
            p = {'w_in': gqa_w_in[j], 'q_norm_w': gqa_q_norm_w[j], 'k_norm_w': gqa_k_norm_w[j],
                 'w_out': gqa_w_out[j]}
            out_p, k_new, v_new = odd_mixer(h_p, p, None, None, None)
            out_s, _, _ = odd_mixer(h_s, p, rope_c, cache_c_k[:, j], cache_c_v[:, j])
            new_c_k.append(k_new)
            new_c_v.append(v_new)
        y_p = y_p + gate_p * out_p
        y_s = y_s + gate_s * out_s

    state_a_k = jnp.stack(new_a_k, axis=1)
    state_a_v = jnp.stack(new_a_v, axis=1)
    state_c_k = jnp.stack(new_c_k, axis=1)
    state_c_v = jnp.stack(new_c_v, axis=1)
    return (y_p, y_s, state_a_k, state_a_v, state_c_k, state_c_v)
```

```python
import functools
import math

import numpy as np
import jax
import jax.numpy as jnp
from jax import lax
from jax.experimental import pallas as pl
from jax.experimental.pallas import tpu as pltpu

F32 = jnp.float32
BF16 = jnp.bfloat16

GRID_W = 64
ROPE_THETA = 10000.0
EPS = 1e-6
H_A = 8
DK_A = 64
DV_A = 2 * DK_A
W_A = H_A * DV_A
POOL_WINDOWS = (2, 4, 8, 16)
N_POOL = len(POOL_WINDOWS)
HD_C = 128
KVH_C = 4

LANES = 128
MXU_DIM = 256
VMEM_LIMIT_BYTES = 56 * 1024 * 1024

TOKEN_TILE = 256
ADA_TILE_N = 1024


def _compiler_params(n_axes):
    return pltpu.CompilerParams(
        dimension_semantics=("arbitrary",) * n_axes, vmem_limit_bytes=VMEM_LIMIT_BYTES)


def _resident(block_shape):
    zeros = (0,) * len(block_shape)
    return pl.BlockSpec(block_shape, lambda *_: zeros, pipeline_mode=pl.Buffered(1))


def _silu(x):
    return x / (1.0 + jnp.exp(-x))


def _softmax_rows(s):
    e = jnp.exp(s - jnp.max(s, axis=-1, keepdims=True))
    return e * (1.0 / jnp.sum(e, axis=-1, keepdims=True))


def _dot(a, b):
    return jnp.dot(a, b, preferred_element_type=F32)


def _dot_nt(a, b):
    return lax.dot_general(a, b, (((1,), (1,)), ((), ())), preferred_element_type=F32)


def _ada_kernel(cond_ref, w_ref, b_ref, o_ref):
    s = _silu(cond_ref[...]).astype(BF16)
    o_ref[0] = _dot(s, w_ref[0].astype(BF16)) + b_ref[0]


def _ada_rows(cond, ada_w, ada_b):
    depth, d, n = ada_w.shape
    rows = cond.shape[0]
    return pl.pallas_call(
        _ada_kernel,
        grid=(depth, n // ADA_TILE_N),
        in_specs=[
            pl.BlockSpec((rows, d), lambda l, j: (0, 0)),
            pl.BlockSpec((1, d, ADA_TILE_N), lambda l, j: (l, 0, j)),
            pl.BlockSpec((1, 1, ADA_TILE_N), lambda l, j: (l, 0, j)),
        ],
        out_specs=pl.BlockSpec((1, rows, ADA_TILE_N), lambda l, j: (l, 0, j)),
        out_shape=jax.ShapeDtypeStruct((depth, rows, n), F32),
        compiler_params=_compiler_params(2),
        name="ada_rows",
    )(cond, ada_w, ada_b.reshape(depth, 1, n))


def _modulated_norm(x, mod_ref, nw_ref):
    r = lax.rsqrt(jnp.mean(x * x, axis=-1, keepdims=True) + EPS)
    a = nw_ref[...] * (1.0 + mod_ref[0, 1:2, :])
    return ((x * r) * a + mod_ref[0, 0:1, :]).astype(BF16)


def _group_rms_scale(t, gsum_ref, group):
    ms = _dot((t * t).astype(BF16), gsum_ref[...]) * (1.0 / group)
    return lax.rsqrt(ms + EPS)


def _rope(t, cos, sin_signed, quarter):
    lane = lax.broadcasted_iota(jnp.int32, t.shape, 1)
    upper = (lane & quarter) != 0
    rot = jnp.where(upper, pltpu.roll(t, quarter, 1), pltpu.roll(t, LANES - quarter, 1))
    return t * cos + rot * sin_signed


def _head_slabs(proj, w_row, gsum_ref, group, rope):
    slabs = []
    for c in range(proj.shape[1] // MXU_DIM):
        t = proj[:, c * MXU_DIM:(c + 1) * MXU_DIM]
        t = t * _group_rms_scale(t, gsum_ref, group)
        for j in range(MXU_DIM // LANES):
            slab = t[:, j * LANES:(j + 1) * LANES] * w_row
            if rope is not None:
                slab = _rope(slab, *rope)
            slabs.append(slab)
    return slabs


def _even_in_kernel(*refs, rope):
    if rope:
        (x_ref, mod_ref, nw_ref, w_ref, qw_ref, kw_ref, gsum_ref, cos_ref, sin_ref,
         q_ref, k_ref, v_ref, u_ref, g_ref) = refs
        rope_args = (cos_ref[...], sin_ref[...], DK_A // 4)
    else:
        (x_ref, mod_ref, nw_ref, w_ref, qw_ref, kw_ref, gsum_ref,
         q_ref, k_ref, v_ref, u_ref, g_ref) = refs
        rope_args = None
    h = _modulated_norm(x_ref[0], mod_ref, nw_ref)
    w_b = u_ref.shape[2]

    q = _dot(h, w_ref[:, 0:W_A])
    for i, slab in enumerate(_head_slabs(q, qw_ref[...], gsum_ref, DK_A, rope_args)):
        q_ref[0, i] = slab.astype(q_ref.dtype)
    k = _dot(h, w_ref[:, W_A:2 * W_A])
    for i, slab in enumerate(_head_slabs(k, kw_ref[...], gsum_ref, DK_A, rope_args)):
        k_ref[0, i] = slab.astype(k_ref.dtype)
    v = _dot(h, w_ref[:, 2 * W_A:3 * W_A])
    for i in range(H_A):
        v_ref[0, i] = v[:, i * DV_A:(i + 1) * DV_A].astype(v_ref.dtype)
    u_ref[0] = _dot(h, w_ref[:, 3 * W_A:3 * W_A + w_b])
    g_ref[0] = _dot(h, w_ref[:, 3 * W_A + w_b:]).astype(g_ref.dtype)


def _even_in(x, mod, norm_w, w_in, qw, kw, gsum, rope, kv_dtype):
    b, l, d = x.shape
    n_in = w_in.shape[1]
    w_b = d - W_A
    tl = TOKEN_TILE
    tok = lambda i, t: (i, t, 0)
    head = lambda i, t: (i, 0, t, 0)
    in_specs = [
        pl.BlockSpec((1, tl, d), tok),
        pl.BlockSpec((1, 3, d), lambda i, t: (i if mod.shape[0] > 1 else 0, 0, 0)),
        _resident((1, d)),
        _resident((d, n_in)),
        _resident((1, LANES)),
        _resident((1, LANES)),
        _resident((MXU_DIM, MXU_DIM)),
    ]
    args = [x, mod, norm_w, w_in, qw, kw, gsum]
    if rope is not None:
        in_specs += [pl.BlockSpec((tl, LANES), lambda i, t: (t, 0))] * 2
        args += list(rope)
    head_shape = (b, H_A, l, DV_A)
    return pl.pallas_call(
        functools.partial(_even_in_kernel, rope=rope is not None),
        grid=(b, l // tl),
        in_specs=in_specs,
        out_specs=[
            pl.BlockSpec((1, H_A, tl, DV_A), head),
            pl.BlockSpec((1, H_A, tl, DV_A), head),
            pl.BlockSpec((1, H_A, tl, DV_A), head),
            pl.BlockSpec((1, tl, w_b), tok),
            pl.BlockSpec((1, tl, d), tok),
        ],
        out_shape=[
            jax.ShapeDtypeStruct(head_shape, BF16),
            jax.ShapeDtypeStruct(head_shape, kv_dtype),
            jax.ShapeDtypeStruct(head_shape, kv_dtype),
            jax.ShapeDtypeStruct((b, l, w_b), F32),
            jax.ShapeDtypeStruct((b, l, d), BF16),
        ],
        compiler_params=_compiler_params(2),
        name="even_in",
    )(*args)


def _even_mix_kernel(*refs, has_ctx, lam_init):
    if has_ctx:
        (q_ref, k_ref, v_ref, ck_ref, cv_ref, u_ref, g_ref, x_ref, mod_ref, band_ref, invc_ref,
         lamv_ref, hmask_ref, subln_ref, poolw_ref, pscale_ref, wout_ref, y_ref, ymix_ref) = refs
    else:
        (q_ref, k_ref, v_ref, u_ref, g_ref, x_ref, mod_ref, band_ref, invc_ref,
         lamv_ref, hmask_ref, subln_ref, poolw_ref, pscale_ref, wout_ref, y_ref, ymix_ref) = refs
    tq = q_ref.shape[2]
    l = u_ref.shape[1]
    gw = poolw_ref.shape[1]

    lv = lamv_ref[...]
    lam = (jnp.exp(jnp.sum(lv[0:1] * lv[1:2], axis=-1, keepdims=True))
           - jnp.exp(jnp.sum(lv[2:3] * lv[3:4], axis=-1, keepdims=True)) + lam_init)

    for i in range(H_A):
        lo, hi = i * DV_A, (i + 1) * DV_A
        qh = q_ref[0, i]
        kh = k_ref[0, i].astype(BF16)
        vh = v_ref[0, i].astype(BF16)
        if has_ctx:
            kh = jnp.concatenate([ck_ref[i].astype(BF16), kh], axis=0)
            vh = jnp.concatenate([cv_ref[i].astype(BF16), vh], axis=0)
        p = (_softmax_rows(_dot_nt(qh * hmask_ref[0:1, :], kh))
             - lam * _softmax_rows(_dot_nt(qh * hmask_ref[1:2, :], kh)))
        o = _dot(p.astype(BF16), vh)
        o = o * lax.rsqrt(jnp.mean(o * o, axis=-1, keepdims=True) + EPS) * subln_ref[...]
        o = o * (1.0 - lam_init)
        ymix_ref[:, lo:hi] = (o * _silu(g_ref[0, :, lo:hi].astype(F32))).astype(BF16)

    if l == tq:
        row0 = 0
    else:
        row0 = pl.multiple_of(pl.program_id(1) * tq, tq)
    for j in range(N_POOL):
        lo, hi = j * gw, (j + 1) * gw
        u_all = u_ref[0, :, lo:hi]
        u_own = u_all if l == tq else u_ref[0, pl.ds(row0, tq), lo:hi]
        d = _dot(band_ref[j], u_all.astype(BF16)) * invc_ref[:, lo:hi] - u_own
        po = _dot(d.astype(BF16), poolw_ref[j]) * pscale_ref[:, lo:hi]
        ymix_ref[:, W_A + lo:W_A + hi] = (
            po * _silu(g_ref[0, :, W_A + lo:W_A + hi].astype(F32))).astype(BF16)

    out = _dot(ymix_ref[...], wout_ref[...])
    y_ref[0] = x_ref[0] + mod_ref[0, 2:3, :] * out


def _even_mix(q, k, v, ctx, u, g, x, mod, band, invc, lamv, hmask, subln, poolw, pscale, w_out,
              lam_init):
    b, l, d = x.shape
    tq = TOKEN_TILE
    w_b = u.shape[2]
    gw = w_b // N_POOL
    tok = lambda i, t: (i, t, 0)
    in_specs = [
        pl.BlockSpec((1, H_A, tq, DV_A), lambda i, t: (i, 0, t, 0)),
        pl.BlockSpec((1, H_A, l, DV_A), lambda i, t: (i, 0, 0, 0)),
        pl.BlockSpec((1, H_A, l, DV_A), lambda i, t: (i, 0, 0, 0)),
    ]
    args = [q, k, v]
    if ctx is not None:
        ck, cv, layer = ctx
        p = ck.shape[3]
        in_specs += [pl.BlockSpec((None, None, H_A, p, DV_A), lambda i, t: (i, layer, 0, 0, 0))] * 2
        args += [ck, cv]
    in_specs += [
        pl.BlockSpec((1, l, w_b), lambda i, t: (i, 0, 0)),
        pl.BlockSpec((1, tq, d), tok),
        pl.BlockSpec((1, tq, d), tok),
        pl.BlockSpec((1, 3, d), lambda i, t: (i if mod.shape[0] > 1 else 0, 0, 0)),
        pl.BlockSpec((N_POOL, tq, l), lambda i, t: (0, t, 0)),
        pl.BlockSpec((tq, w_b), lambda i, t: (t, 0)),
        _resident(lamv.shape),
        _resident(hmask.shape),
        _resident((1, DV_A)),
        _resident((N_POOL, gw, gw)),
        _resident((1, w_b)),
        _resident((d, d)),
    ]
    args += [u, g, x, mod, band, invc, lamv, hmask, subln, poolw, pscale, w_out]
    return pl.pallas_call(
        functools.partial(_even_mix_kernel, has_ctx=ctx is not None, lam_init=lam_init),
        grid=(b, l // tq),
        in_specs=in_specs,
        out_specs=pl.BlockSpec((1, tq, d), tok),
        out_shape=jax.ShapeDtypeStruct((b, l, d), F32),
        scratch_shapes=[pltpu.VMEM((tq, d), BF16)],
        compiler_params=_compiler_params(2),
        name="even_mix",
    )(*args)


def _odd_in_kernel(*refs, rope):
    if rope:
        (x_ref, mod_ref, nw_ref, w_ref, qw_ref, kw_ref, gsum_ref, cos_ref, sin_ref,
         q_ref, k_ref, v_ref, g_ref) = refs
        rope_args = (cos_ref[...], sin_ref[...], HD_C // 4)
    else:
        (x_ref, mod_ref, nw_ref, w_ref, qw_ref, kw_ref, gsum_ref,
         q_ref, k_ref, v_ref, g_ref) = refs
        rope_args = None
    h = _modulated_norm(x_ref[0], mod_ref, nw_ref)
    w_c = g_ref.shape[2]
    w_kv = KVH_C * HD_C

    q = _dot(h, w_ref[:, 0:w_c])
    for i, slab in enumerate(_head_slabs(q, qw_ref[...], gsum_ref, HD_C, rope_args)):
        q_ref[0, i] = slab.astype(q_ref.dtype)
    k = _dot(h, w_ref[:, w_c:w_c + w_kv])
    for i, slab in enumerate(_head_slabs(k, kw_ref[...], gsum_ref, HD_C, rope_args)):
        k_ref[0, i] = slab.astype(k_ref.dtype)
    v = _dot(h, w_ref[:, w_c + w_kv:w_c + 2 * w_kv])
    for i in range(KVH_C):
        v_ref[0, i] = v[:, i * HD_C:(i + 1) * HD_C].astype(v_ref.dtype)
    g_ref[0] = _dot(h, w_ref[:, w_c + 2 * w_kv:]).astype(g_ref.dtype)


def _odd_in(x, mod, norm_w, w_in, qw, kw, gsum, rope, kv_dtype):
    b, l, d = x.shape
    n_in = w_in.shape[1]
    h_c = d // HD_C
    tl = TOKEN_TILE
    tok = lambda i, t: (i, t, 0)
    head = lambda i, t: (i, 0, t, 0)
    in_specs = [
        pl.BlockSpec((1, tl, d), tok),
        pl.BlockSpec((1, 3, d), lambda i, t: (i if mod.shape[0] > 1 else 0, 0, 0)),
        _resident((1, d)),
        _resident((d, n_in)),
        _resident((1, LANES)),
        _resident((1, LANES)),
        _resident((MXU_DIM, MXU_DIM)),
    ]
    args = [x, mod, norm_w, w_in, qw, kw, gsum]
    if rope is not None:
        in_specs += [pl.BlockSpec((tl, LANES), lambda i, t: (t, 0))] * 2
        args += list(rope)
    return pl.pallas_call(
        functools.partial(_odd_in_kernel, rope=rope is not None),
        grid=(b, l // tl),
        in_specs=in_specs,
        out_specs=[
            pl.BlockSpec((1, h_c, tl, HD_C), head),
            pl.BlockSpec((1, KVH_C, tl, HD_C), head),
            pl.BlockSpec((1, KVH_C, tl, HD_C), head),
            pl.BlockSpec((1, tl, d), tok),
        ],
        out_shape=[
            jax.ShapeDtypeStruct((b, h_c, l, HD_C), BF16),
            jax.ShapeDtypeStruct((b, KVH_C, l, HD_C), kv_dtype),
            jax.ShapeDtypeStruct((b, KVH_C, l, HD_C), kv_dtype),
            jax.ShapeDtypeStruct((b, l, d), BF16),
        ],
        compiler_params=_compiler_params(2),
        name="odd_in",
    )(*args)


def _odd_mix_kernel(*refs, has_ctx):
    if has_ctx:
        q_ref, k_ref, v_ref, ck_ref, cv_ref, g_ref, x_ref, mod_ref, wout_ref, y_ref, ymix_ref = refs
    else:
        q_ref, k_ref, v_ref, g_ref, x_ref, mod_ref, wout_ref, y_ref, ymix_ref = refs
    h_c, tq = q_ref.shape[1], q_ref.shape[2]
    rep = h_c // KVH_C

    for j in range(KVH_C):
        kh = k_ref[0, j].astype(BF16)
        vh = v_ref[0, j].astype(BF16)
        if has_ctx:
            kh = jnp.concatenate([ck_ref[j].astype(BF16), kh], axis=0)
            vh = jnp.concatenate([cv_ref[j].astype(BF16), vh], axis=0)
        qs = q_ref[0, j * rep:(j + 1) * rep].reshape(rep * tq, HD_C)
        o = _dot(_softmax_rows(_dot_nt(qs, kh)).astype(BF16), vh)
        for r in range(rep):
            lo, hi = (j * rep + r) * HD_C, (j * rep + r + 1) * HD_C
            ymix_ref[:, lo:hi] = (
                o[r * tq:(r + 1) * tq] * _silu(g_ref[0, :, lo:hi].astype(F32))).astype(BF16)

    out = _dot(ymix_ref[...], wout_ref[...])
    y_ref[0] = x_ref[0] + mod_ref[0, 2:3, :] * out


def _odd_mix(q, k, v, ctx, g, x, mod, w_out):
    b, l, d = x.shape
    h_c = q.shape[1]
    tq = TOKEN_TILE
    tok = lambda i, t: (i, t, 0)
    in_specs = [
        pl.BlockSpec((1, h_c, tq, HD_C), lambda i, t: (i, 0, t, 0)),
        pl.BlockSpec((1, KVH_C, l, HD_C), lambda i, t: (i, 0, 0, 0)),
        pl.BlockSpec((1, KVH_C, l, HD_C), lambda i, t: (i, 0, 0, 0)),
    ]
    args = [q, k, v]
    if ctx is not None:
        ck, cv, layer = ctx
        p = ck.shape[3]
        in_specs += [pl.BlockSpec((None, None, KVH_C, p, HD_C), lambda i, t: (i, layer, 0, 0, 0))] * 2
        args += [ck, cv]
    in_specs += [
        pl.BlockSpec((1, tq, d), tok),
        pl.BlockSpec((1, tq, d), tok),
        pl.BlockSpec((1, 3, d), lambda i, t: (i if mod.shape[0] > 1 else 0, 0, 0)),
        _resident((d, d)),
    ]
    args += [g, x, mod, w_out]
    return pl.pallas_call(
        functools.partial(_odd_mix_kernel, has_ctx=ctx is not None),
        grid=(b, l // tq),
        in_specs=in_specs,
        out_specs=pl.BlockSpec((1, tq, d), tok),
        out_shape=jax.ShapeDtypeStruct((b, l, d), F32),
        scratch_shapes=[pltpu.VMEM((tq, d), BF16)],
        compiler_params=_compiler_params(2),
        name="odd_mix",
    )(*args)


def _rope_tables(n_tokens, dim):
    rows = n_tokens // GRID_W
    row = jnp.repeat(jnp.arange(rows), GRID_W).astype(F32)
    col = jnp.tile(jnp.arange(GRID_W), rows).astype(F32)
    quarter = dim // 4
    freqs = ROPE_THETA ** (-jnp.arange(quarter, dtype=F32) / quarter)
    ar = row[:, None] * freqs
    ac = col[:, None] * freqs
    cos = jnp.concatenate([jnp.cos(ar), jnp.cos(ar), jnp.cos(ac), jnp.cos(ac)], axis=-1)
    sin = jnp.concatenate([-jnp.sin(ar), jnp.sin(ar), -jnp.sin(ac), jnp.sin(ac)], axis=-1)
    reps = LANES // dim
    return jnp.tile(cos, (1, reps)), jnp.tile(sin, (1, reps))


def _group_sum_matrix(group):
    idx = np.arange(MXU_DIM) // group
    return jnp.asarray(idx[:, None] == idx[None, :], dtype=BF16)


def _pool_tables(l, gw):
    t = np.arange(l)[:, None]
    s = np.arange(l)[None, :]
    bands, invs = [], []
    for w in POOL_WINDOWS:
        lo = np.clip(t - w // 2, 0, l)
        hi = np.clip(t + w // 2, 0, l)
        bands.append((s >= lo) & (s < hi))
        invs.append(np.repeat(1.0 / (hi - lo).astype(np.float32), gw, axis=1))
    return (jnp.asarray(np.stack(bands), dtype=BF16),
            jnp.asarray(np.concatenate(invs, axis=1), dtype=F32))


def kernel(x_prompt, x_sample, cache_a_k, cache_a_v, cache_c_k, cache_c_v, c, c_ctx, norm_w, ada_w, ada_b, even_w_in, even_q_norm_w, even_k_norm_w, even_lam_q1, even_lam_k1, even_lam_q2, even_lam_k2, even_subln_w, even_pool_w, even_pool_scale, even_w_out, gqa_w_in, gqa_q_norm_w, gqa_k_norm_w, gqa_w_out):
    depth, d = norm_w.shape
    n_lat = x_sample.shape[1]
    n_dec = x_sample.shape[0]
    w_b = d - W_A
    gw = w_b // N_POOL

    rows = -(-(1 + n_dec) // 8) * 8
    cond = jnp.zeros((rows, d), F32).at[0].set(c_ctx).at[1:1 + n_dec].set(c)
    mod = _ada_rows(cond, ada_w, ada_b).reshape(depth, rows, 3, d)

    rope_a = _rope_tables(n_lat, DK_A)
    rope_c = _rope_tables(n_lat, HD_C)
    gsum_a = _group_sum_matrix(DK_A)
    gsum_c = _group_sum_matrix(HD_C)
    pool_p = _pool_tables(x_prompt.shape[1], gw)
    pool_s = _pool_tables(n_lat, gw)
    hmask = jnp.asarray(np.arange(LANES)[None, :] // DK_A == np.arange(2)[:, None], dtype=BF16)

    y_p, y_s = x_prompt, x_sample
    new_a_k, new_a_v, new_c_k, new_c_v = [], [], [], []
    for i in range(depth):
        j = i // 2
        mod_p = mod[i, 0:1]
        mod_s = mod[i, 1:1 + n_dec]
        nw = norm_w[i].reshape(1, d)
        if i % 2 == 0:
            lam_init = 0.8 - 0.6 * math.exp(-0.3 * i)
            w_in = even_w_in[j].astype(BF16)
            w_out = even_w_out[j].astype(BF16)
            poolw = even_pool_w[j].astype(BF16)
            qw = jnp.tile(even_q_norm_w[j], LANES // DK_A).reshape(1, LANES) * (DK_A ** -0.5)
            kw = jnp.tile(even_k_norm_w[j], LANES // DK_A).reshape(1, LANES)
            lamv = jnp.stack([even_lam_q1[j], even_lam_k1[j], even_lam_q2[j], even_lam_k2[j]])
            subln = even_subln_w[j].reshape(1, DV_A)
            pscale = even_pool_scale[j].reshape(1, w_b)
            shared = (lamv, hmask, subln, poolw, pscale, w_out, lam_init)

            q, k, v, u, g = _even_in(y_p, mod_p, nw, w_in, qw, kw, gsum_a, None, F32)
            new_a_k.append(k)
            new_a_v.append(v)
            y_p = _even_mix(q, k, v, None, u, g, y_p, mod_p, *pool_p, *shared)
            q, k, v, u, g = _even_in(y_s, mod_s, nw, w_in, qw, kw, gsum_a, rope_a, BF16)
            y_s = _even_mix(q, k, v, (cache_a_k, cache_a_v, j), u, g, y_s, mod_s, *pool_s, *shared)
        else:
            w_in = gqa_w_in[j].astype(BF16)
            w_out = gqa_w_out[j].astype(BF16)
            qw = gqa_q_norm_w[j].reshape(1, HD_C) * (HD_C ** -0.5)
            kw = gqa_k_norm_w[j].reshape(1, HD_C)

            q, k, v, g = _odd_in(y_p, mod_p, nw, w_in, qw, kw, gsum_c, None, F32)
            new_c_k.append(k)
            new_c_v.append(v)
            y_p = _odd_mix(q, k, v, None, g, y_p, mod_p, w_out)
            q, k, v, g = _odd_in(y_s, mod_s, nw, w_in, qw, kw, gsum_c, rope_c, BF16)
            y_s = _odd_mix(q, k, v, (cache_c_k, cache_c_v, j), g, y_s, mod_s, w_out)

    return (y_p, y_s, jnp.stack(new_a_k, axis=1), jnp.stack(new_a_v, axis=1),
            jnp.stack(new_c_k, axis=1), jnp.stack(new_c_v, axis=1))
```

```python
import functools
import math

import numpy as np
import jax
import jax.numpy as jnp
from jax import lax
from jax.experimental import pallas as pl
from jax.experimental.pallas import tpu as pltpu

F32 = jnp.float32
BF16 = jnp.bfloat16

GRID_W = 64
ROPE_THETA = 10000.0
EPS = 1e-6
H_A = 8
DK_A = 64
DV_A = 2 * DK_A
W_A = H_A * DV_A
POOL_WINDOWS = (2, 4, 8, 16)
N_POOL = len(POOL_WINDOWS)
HD_C = 128
KVH_C = 4

LANES = 128
MXU_DIM = 256
VMEM_LIMIT_BYTES = 56 * 1024 * 1024

TOKEN_TILE = 256
ADA_TILE_N = 1024
SCORE_BYTES_PER_GROUP = 6 * 1024 * 1024


def _heads_per_group(n_heads, score_elems_per_head):
    per = n_heads
    while per > 1 and per * score_elems_per_head * 4 > SCORE_BYTES_PER_GROUP:
        per //= 2
    return per


def _compiler_params(n_axes):
    return pltpu.CompilerParams(
        dimension_semantics=("arbitrary",) * n_axes, vmem_limit_bytes=VMEM_LIMIT_BYTES)


def _resident(block_shape):
    zeros = (0,) * len(block_shape)
    return pl.BlockSpec(block_shape, lambda *_: zeros, pipeline_mode=pl.Buffered(1))


def _silu(x):
    return x / (1.0 + jnp.exp(-x))


def _softmax_rows(s):
    e = jnp.exp(s - jnp.max(s, axis=-1, keepdims=True))
    return e * (1.0 / jnp.sum(e, axis=-1, keepdims=True))


def _dot(a, b):
    return jnp.dot(a, b, preferred_element_type=F32)


def _dot_nt(a, b):
    return lax.dot_general(a, b, (((1,), (1,)), ((), ())), preferred_element_type=F32)


def _ada_kernel(cond_ref, w_ref, b_ref, o_ref):
    s = _silu(cond_ref[...]).astype(BF16)
    o_ref[0] = _dot(s, w_ref[0].astype(BF16)) + b_ref[0]


def _ada_rows(cond, ada_w, ada_b):
    depth, d, n = ada_w.shape
    rows = cond.shape[0]
    return pl.pallas_call(
        _ada_kernel,
        grid=(depth, n // ADA_TILE_N),
        in_specs=[
            pl.BlockSpec((rows, d), lambda l, j: (0, 0)),
            pl.BlockSpec((1, d, ADA_TILE_N), lambda l, j: (l, 0, j)),
            pl.BlockSpec((1, 1, ADA_TILE_N), lambda l, j: (l, 0, j)),
        ],
        out_specs=pl.BlockSpec((1, rows, ADA_TILE_N), lambda l, j: (l, 0, j)),
        out_shape=jax.ShapeDtypeStruct((depth, rows, n), F32),
        compiler_params=_compiler_params(2),
        name="ada_rows",
    )(cond, ada_w, ada_b.reshape(depth, 1, n))


def _modulated_norm(x, mod_ref, nw_ref):
    r = lax.rsqrt(jnp.mean(x * x, axis=-1, keepdims=True) + EPS)
    a = nw_ref[...] * (1.0 + mod_ref[0, 1:2, :])
    return ((x * r) * a + mod_ref[0, 0:1, :]).astype(BF16)


def _group_rms_scale(t, gsum_ref, group):
    ms = _dot((t * t).astype(BF16), gsum_ref[...]) * (1.0 / group)
    return lax.rsqrt(ms + EPS)


def _rope(t, cos, sin_signed, quarter):
    lane = lax.broadcasted_iota(jnp.int32, t.shape, 1)
    upper = (lane & quarter) != 0
    rot = jnp.where(upper, pltpu.roll(t, quarter, 1), pltpu.roll(t, LANES - quarter, 1))
    return t * cos + rot * sin_signed


def _head_slabs(proj, w_row, gsum_ref, group, rope):
    slabs = []
    for c in range(proj.shape[1] // MXU_DIM):
        t = proj[:, c * MXU_DIM:(c + 1) * MXU_DIM]
        t = t * _group_rms_scale(t, gsum_ref, group)
        for j in range(MXU_DIM // LANES):
            slab = t[:, j * LANES:(j + 1) * LANES] * w_row
            if rope is not None:
                slab = _rope(slab, *rope)
            slabs.append(slab)
    return slabs


def _even_in_kernel(*refs, rope):
    if rope:
        (x_ref, mod_ref, nw_ref, w_ref, qw_ref, kw_ref, gsum_ref, cos_ref, sin_ref,
         q_ref, k_ref, v_ref, u_ref, g_ref) = refs
        rope_args = (cos_ref[...], sin_ref[...], DK_A // 4)
    else:
        (x_ref, mod_ref, nw_ref, w_ref, qw_ref, kw_ref, gsum_ref,
         q_ref, k_ref, v_ref, u_ref, g_ref) = refs
        rope_args = None
    h = _modulated_norm(x_ref[0], mod_ref, nw_ref)
    w_b = u_ref.shape[2]

    q = _dot(h, w_ref[:, 0:W_A])
    for i, slab in enumerate(_head_slabs(q, qw_ref[...], gsum_ref, DK_A, rope_args)):
        q_ref[0, i] = slab.astype(q_ref.dtype)
    k = _dot(h, w_ref[:, W_A:2 * W_A])
    for i, slab in enumerate(_head_slabs(k, kw_ref[...], gsum_ref, DK_A, rope_args)):
        k_ref[0, i] = slab.astype(k_ref.dtype)
    v = _dot(h, w_ref[:, 2 * W_A:3 * W_A])
    for i in range(H_A):
        v_ref[0, i] = v[:, i * DV_A:(i + 1) * DV_A].astype(v_ref.dtype)
    u_ref[0] = _dot(h, w_ref[:, 3 * W_A:3 * W_A + w_b])
    g_ref[0] = _dot(h, w_ref[:, 3 * W_A + w_b:]).astype(g_ref.dtype)


def _even_in(x, mod, norm_w, w_in, qw, kw, gsum, rope, kv_dtype):
    b, l, d = x.shape
    n_in = w_in.shape[1]
    w_b = d - W_A
    tl = TOKEN_TILE
    tok = lambda i, t: (i, t, 0)
    head = lambda i, t: (i, 0, t, 0)
    in_specs = [
        pl.BlockSpec((1, tl, d), tok),
        pl.BlockSpec((1, 3, d), lambda i, t: (i if mod.shape[0] > 1 else 0, 0, 0)),
        _resident((1, d)),
        _resident((d, n_in)),
        _resident((1, LANES)),
        _resident((1, LANES)),
        _resident((MXU_DIM, MXU_DIM)),
    ]
    args = [x, mod, norm_w, w_in, qw, kw, gsum]
    if rope is not None:
        in_specs += [pl.BlockSpec((tl, LANES), lambda i, t: (t, 0))] * 2
        args += list(rope)
    head_shape = (b, H_A, l, DV_A)
    return pl.pallas_call(
        functools.partial(_even_in_kernel, rope=rope is not None),
        grid=(b, l // tl),
        in_specs=in_specs,
        out_specs=[
            pl.BlockSpec((1, H_A, tl, DV_A), head),
            pl.BlockSpec((1, H_A, tl, DV_A), head),
            pl.BlockSpec((1, H_A, tl, DV_A), head),
            pl.BlockSpec((1, tl, w_b), tok),
            pl.BlockSpec((1, tl, d), tok),
        ],
        out_shape=[
            jax.ShapeDtypeStruct(head_shape, BF16),
            jax.ShapeDtypeStruct(head_shape, kv_dtype),
            jax.ShapeDtypeStruct(head_shape, kv_dtype),
            jax.ShapeDtypeStruct((b, l, w_b), F32),
            jax.ShapeDtypeStruct((b, l, d), BF16),
        ],
        compiler_params=_compiler_params(2),
        name="even_in",
    )(*args)


def _even_mix_kernel(*refs, has_ctx, lam_init):
    if has_ctx:
        (q_ref, k_ref, v_ref, ck_ref, cv_ref, u_ref, g_ref, x_ref, mod_ref, band_ref, invc_ref,
         lamv_ref, hmask_ref, subln_ref, poolw_ref, pscale_ref, wout_ref, y_ref, ymix_ref) = refs
    else:
        (q_ref, k_ref, v_ref, u_ref, g_ref, x_ref, mod_ref, band_ref, invc_ref,
         lamv_ref, hmask_ref, subln_ref, poolw_ref, pscale_ref, wout_ref, y_ref, ymix_ref) = refs
    tq = q_ref.shape[2]
    l = u_ref.shape[1]
    gw = poolw_ref.shape[1]

    lv = lamv_ref[...]
    lam = (jnp.exp(jnp.sum(lv[0:1] * lv[1:2], axis=-1, keepdims=True))
           - jnp.exp(jnp.sum(lv[2:3] * lv[3:4], axis=-1, keepdims=True)) + lam_init)

    n_keys = k_ref.shape[2] + (ck_ref.shape[1] if has_ctx else 0)
    per_group = _heads_per_group(H_A, 2 * tq * n_keys)
    for first in range(0, H_A, per_group):
        heads = range(first, first + per_group)
        scores, values = [], []
        for i in heads:
            qh = q_ref[0, i]
            kh = k_ref[0, i].astype(BF16)
            vh = v_ref[0, i].astype(BF16)
            if has_ctx:
                kh = jnp.concatenate([ck_ref[i].astype(BF16), kh], axis=0)
                vh = jnp.concatenate([cv_ref[i].astype(BF16), vh], axis=0)
            scores.append((_dot_nt(qh * hmask_ref[0:1, :], kh), _dot_nt(qh * hmask_ref[1:2, :], kh)))
            values.append(vh)
        probs = [(_softmax_rows(s1) - lam * _softmax_rows(s2)).astype(BF16) for s1, s2 in scores]
        outs = [_dot(p, vh) for p, vh in zip(probs, values)]
        for i, o in zip(heads, outs):
            lo, hi = i * DV_A, (i + 1) * DV_A
            o = o * lax.rsqrt(jnp.mean(o * o, axis=-1, keepdims=True) + EPS) * subln_ref[...]
            o = o * (1.0 - lam_init)
            ymix_ref[:, lo:hi] = (o * _silu(g_ref[0, :, lo:hi].astype(F32))).astype(BF16)

    if l == tq:
        row0 = 0
    else:
        row0 = pl.multiple_of(pl.program_id(1) * tq, tq)
    for j in range(N_POOL):
        lo, hi = j * gw, (j + 1) * gw
        u_all = u_ref[0, :, lo:hi]
        u_own = u_all if l == tq else u_ref[0, pl.ds(row0, tq), lo:hi]
        d = _dot(band_ref[j], u_all.astype(BF16)) * invc_ref[:, lo:hi] - u_own
        po = _dot(d.astype(BF16), poolw_ref[j]) * pscale_ref[:, lo:hi]
        ymix_ref[:, W_A + lo:W_A + hi] = (
            po * _silu(g_ref[0, :, W_A + lo:W_A + hi].astype(F32))).astype(BF16)

    out = _dot(ymix_ref[...], wout_ref[...])
    y_ref[0] = x_ref[0] + mod_ref[0, 2:3, :] * out


def _even_mix(q, k, v, ctx, u, g, x, mod, band, invc, lamv, hmask, subln, poolw, pscale, w_out,
              lam_init):
    b, l, d = x.shape
    tq = TOKEN_TILE
    w_b = u.shape[2]
    gw = w_b // N_POOL
    tok = lambda i, t: (i, t, 0)
    in_specs = [
        pl.BlockSpec((1, H_A, tq, DV_A), lambda i, t: (i, 0, t, 0)),
        pl.BlockSpec((1, H_A, l, DV_A), lambda i, t: (i, 0, 0, 0)),
        pl.BlockSpec((1, H_A, l, DV_A), lambda i, t: (i, 0, 0, 0)),
    ]
    args = [q, k, v]
    if ctx is not None:
        ck, cv, layer = ctx
        p = ck.shape[3]
        in_specs += [pl.BlockSpec((None, None, H_A, p, DV_A), lambda i, t: (i, layer, 0, 0, 0))] * 2
        args += [ck, cv]
    in_specs += [
        pl.BlockSpec((1, l, w_b), lambda i, t: (i, 0, 0)),
        pl.BlockSpec((1, tq, d), tok),
        pl.BlockSpec((1, tq, d), tok),
        pl.BlockSpec((1, 3, d), lambda i, t: (i if mod.shape[0] > 1 else 0, 0, 0)),
        pl.BlockSpec((N_POOL, tq, l), lambda i, t: (0, t, 0)),
        pl.BlockSpec((tq, w_b), lambda i, t: (t, 0)),
        _resident(lamv.shape),
        _resident(hmask.shape),
        _resident((1, DV_A)),
        _resident((N_POOL, gw, gw)),
        _resident((1, w_b)),
        _resident((d, d)),
    ]
    args += [u, g, x, mod, band, invc, lamv, hmask, subln, poolw, pscale, w_out]
    return pl.pallas_call(
        functools.partial(_even_mix_kernel, has_ctx=ctx is not None, lam_init=lam_init),
        grid=(b, l // tq),
        in_specs=in_specs,
        out_specs=pl.BlockSpec((1, tq, d), tok),
        out_shape=jax.ShapeDtypeStruct((b, l, d), F32),
        scratch_shapes=[pltpu.VMEM((tq, d), BF16)],
        compiler_params=_compiler_params(2),
        name="even_mix",
    )(*args)


def _odd_in_kernel(*refs, rope):
    if rope:
        (x_ref, mod_ref, nw_ref, w_ref, qw_ref, kw_ref, gsum_ref, cos_ref, sin_ref,
         q_ref, k_ref, v_ref, g_ref) = refs
        rope_args = (cos_ref[...], sin_ref[...], HD_C // 4)
    else:
        (x_ref, mod_ref, nw_ref, w_ref, qw_ref, kw_ref, gsum_ref,
         q_ref, k_ref, v_ref, g_ref) = refs
        rope_args = None
    h = _modulated_norm(x_ref[0], mod_ref, nw_ref)
    w_c = g_ref.shape[2]
    w_kv = KVH_C * HD_C

    q = _dot(h, w_ref[:, 0:w_c])
    for i, slab in enumerate(_head_slabs(q, qw_ref[...], gsum_ref, HD_C, rope_args)):
        q_ref[0, i] = slab.astype(q_ref.dtype)
    k = _dot(h, w_ref[:, w_c:w_c + w_kv])
    for i, slab in enumerate(_head_slabs(k, kw_ref[...], gsum_ref, HD_C, rope_args)):
        k_ref[0, i] = slab.astype(k_ref.dtype)
    v = _dot(h, w_ref[:, w_c + w_kv:w_c + 2 * w_kv])
    for i in range(KVH_C):
        v_ref[0, i] = v[:, i * HD_C:(i + 1) * HD_C].astype(v_ref.dtype)
    g_ref[0] = _dot(h, w_ref[:, w_c + 2 * w_kv:]).astype(g_ref.dtype)


def _odd_in(x, mod, norm_w, w_in, qw, kw, gsum, rope, kv_dtype):
    b, l, d = x.shape
    n_in = w_in.shape[1]
    h_c = d // HD_C
    tl = TOKEN_TILE
    tok = lambda i, t: (i, t, 0)
    head = lambda i, t: (i, 0, t, 0)
    in_specs = [
        pl.BlockSpec((1, tl, d), tok),
        pl.BlockSpec((1, 3, d), lambda i, t: (i if mod.shape[0] > 1 else 0, 0, 0)),
        _resident((1, d)),
        _resident((d, n_in)),
        _resident((1, LANES)),
        _resident((1, LANES)),
        _resident((MXU_DIM, MXU_DIM)),
    ]
    args = [x, mod, norm_w, w_in, qw, kw, gsum]
    if rope is not None:
        in_specs += [pl.BlockSpec((tl, LANES), lambda i, t: (t, 0))] * 2
        args += list(rope)
    return pl.pallas_call(
        functools.partial(_odd_in_kernel, rope=rope is not None),
        grid=(b, l // tl),
        in_specs=in_specs,
        out_specs=[
            pl.BlockSpec((1, h_c, tl, HD_C), head),
            pl.BlockSpec((1, KVH_C, tl, HD_C), head),
            pl.BlockSpec((1, KVH_C, tl, HD_C), head),
            pl.BlockSpec((1, tl, d), tok),
        ],
        out_shape=[
            jax.ShapeDtypeStruct((b, h_c, l, HD_C), BF16),
            jax.ShapeDtypeStruct((b, KVH_C, l, HD_C), kv_dtype),
            jax.ShapeDtypeStruct((b, KVH_C, l, HD_C), kv_dtype),
            jax.ShapeDtypeStruct((b, l, d), BF16),
        ],
        compiler_params=_compiler_params(2),
        name="odd_in",
    )(*args)


def _odd_mix_kernel(*refs, has_ctx):
    if has_ctx:
        q_ref, k_ref, v_ref, ck_ref, cv_ref, g_ref, x_ref, mod_ref, wout_ref, y_ref, ymix_ref = refs
    else:
        q_ref, k_ref, v_ref, g_ref, x_ref, mod_ref, wout_ref, y_ref, ymix_ref = refs
    h_c, tq = q_ref.shape[1], q_ref.shape[2]
    rep = h_c // KVH_C

    n_keys = k_ref.shape[2] + (ck_ref.shape[1] if has_ctx else 0)
    per_group = _heads_per_group(KVH_C, rep * tq * n_keys)
    for first in range(0, KVH_C, per_group):
        kv_heads = range(first, first + per_group)
        scores, values = [], []
        for j in kv_heads:
            kh = k_ref[0, j].astype(BF16)
            vh = v_ref[0, j].astype(BF16)
            if has_ctx:
                kh = jnp.concatenate([ck_ref[j].astype(BF16), kh], axis=0)
                vh = jnp.concatenate([cv_ref[j].astype(BF16), vh], axis=0)
            qs = q_ref[0, j * rep:(j + 1) * rep].reshape(rep * tq, HD_C)
            scores.append(_dot_nt(qs, kh))
            values.append(vh)
        probs = [_softmax_rows(s).astype(BF16) for s in scores]
        outs = [_dot(p, vh) for p, vh in zip(probs, values)]
        for j, o in zip(kv_heads, outs):
            for r in range(rep):
                lo, hi = (j * rep + r) * HD_C, (j * rep + r + 1) * HD_C
                ymix_ref[:, lo:hi] = (
                    o[r * tq:(r + 1) * tq] * _silu(g_ref[0, :, lo:hi].astype(F32))).astype(BF16)

    out = _dot(ymix_ref[...], wout_ref[...])
    y_ref[0] = x_ref[0] + mod_ref[0, 2:3, :] * out


def _odd_mix(q, k, v, ctx, g, x, mod, w_out):
    b, l, d = x.shape
    h_c = q.shape[1]
    tq = TOKEN_TILE
    tok = lambda i, t: (i, t, 0)
    in_specs = [
        pl.BlockSpec((1, h_c, tq, HD_C), lambda i, t: (i, 0, t, 0)),
        pl.BlockSpec((1, KVH_C, l, HD_C), lambda i, t: (i, 0, 0, 0)),
        pl.BlockSpec((1, KVH_C, l, HD_C), lambda i, t: (i, 0, 0, 0)),
    ]
    args = [q, k, v]
    if ctx is not None:
        ck, cv, layer = ctx
        p = ck.shape[3]
        in_specs += [pl.BlockSpec((None, None, KVH_C, p, HD_C), lambda i, t: (i, layer, 0, 0, 0))] * 2
        args += [ck, cv]
    in_specs += [
        pl.BlockSpec((1, tq, d), tok),
        pl.BlockSpec((1, tq, d), tok),
        pl.BlockSpec((1, 3, d), lambda i, t: (i if mod.shape[0] > 1 else 0, 0, 0)),
        _resident((d, d)),
    ]
    args += [g, x, mod, w_out]
    return pl.pallas_call(
        functools.partial(_odd_mix_kernel, has_ctx=ctx is not None),
        grid=(b, l // tq),
        in_specs=in_specs,
        out_specs=pl.BlockSpec((1, tq, d), tok),
        out_shape=jax.ShapeDtypeStruct((b, l, d), F32),
        scratch_shapes=[pltpu.VMEM((tq, d), BF16)],
        compiler_params=_compiler_params(2),
        name="odd_mix",
    )(*args)


def _rope_tables(n_tokens, dim):
    rows = n_tokens // GRID_W
    row = jnp.repeat(jnp.arange(rows), GRID_W).astype(F32)
    col = jnp.tile(jnp.arange(GRID_W), rows).astype(F32)
    quarter = dim // 4
    freqs = ROPE_THETA ** (-jnp.arange(quarter, dtype=F32) / quarter)
    ar = row[:, None] * freqs
    ac = col[:, None] * freqs
    cos = jnp.concatenate([jnp.cos(ar), jnp.cos(ar), jnp.cos(ac), jnp.cos(ac)], axis=-1)
    sin = jnp.concatenate([-jnp.sin(ar), jnp.sin(ar), -jnp.sin(ac), jnp.sin(ac)], axis=-1)
    reps = LANES // dim
    return jnp.tile(cos, (1, reps)), jnp.tile(sin, (1, reps))


def _group_sum_matrix(group):
    idx = np.arange(MXU_DIM) // group
    return jnp.asarray(idx[:, None] == idx[None, :], dtype=BF16)


def _pool_tables(l, gw):
    t = np.arange(l)[:, None]
    s = np.arange(l)[None, :]
    bands, invs = [], []
    for w in POOL_WINDOWS:
        lo = np.clip(t - w // 2, 0, l)
        hi = np.clip(t + w // 2, 0, l)
        bands.append((s >= lo) & (s < hi))
        invs.append(np.repeat(1.0 / (hi - lo).astype(np.float32), gw, axis=1))
    return (jnp.asarray(np.stack(bands), dtype=BF16),
            jnp.asarray(np.concatenate(invs, axis=1), dtype=F32))


def kernel(x_prompt, x_sample, cache_a_k, cache_a_v, cache_c_k, cache_c_v, c, c_ctx, norm_w, ada_w, ada_b, even_w_in, even_q_norm_w, even_k_norm_w, even_lam_q1, even_lam_k1, even_lam_q2, even_lam_k2, even_subln_w, even_pool_w, even_pool_scale, even_w_out, gqa_w_in, gqa_q_norm_w, gqa_k_norm_w, gqa_w_out):
    depth, d = norm_w.shape
    n_lat = x_sample.shape[1]
    n_dec = x_sample.shape[0]
    w_b = d - W_A
    gw = w_b // N_POOL

    rows = -(-(1 + n_dec) // 8) * 8
    cond = jnp.zeros((rows, d), F32).at[0].set(c_ctx).at[1:1 + n_dec].set(c)
    mod = _ada_rows(cond, ada_w, ada_b).reshape(depth, rows, 3, d)

    rope_a = _rope_tables(n_lat, DK_A)
    rope_c = _rope_tables(n_lat, HD_C)
    gsum_a = _group_sum_matrix(DK_A)
    gsum_c = _group_sum_matrix(HD_C)
    pool_p = _pool_tables(x_prompt.shape[1], gw)
    pool_s = _pool_tables(n_lat, gw)
    hmask = jnp.asarray(np.arange(LANES)[None, :] // DK_A == np.arange(2)[:, None], dtype=BF16)

    y_p, y_s = x_prompt, x_sample
    new_a_k, new_a_v, new_c_k, new_c_v = [], [], [], []
    for i in range(depth):
        j = i // 2
        mod_p = mod[i, 0:1]
        mod_s = mod[i, 1:1 + n_dec]
        nw = norm_w[i].reshape(1, d)
        if i % 2 == 0:
            lam_init = 0.8 - 0.6 * math.exp(-0.3 * i)
            w_in = even_w_in[j].astype(BF16)
            w_out = even_w_out[j].astype(BF16)
            poolw = even_pool_w[j].astype(BF16)
            qw = jnp.tile(even_q_norm_w[j], LANES // DK_A).reshape(1, LANES) * (DK_A ** -0.5)
            kw = jnp.tile(even_k_norm_w[j], LANES // DK_A).reshape(1, LANES)
            lamv = jnp.stack([even_lam_q1[j], even_lam_k1[j], even_lam_q2[j], even_lam_k2[j]])
            subln = even_subln_w[j].reshape(1, DV_A)
            pscale = even_pool_scale[j].reshape(1, w_b)
            shared = (lamv, hmask, subln, poolw, pscale, w_out, lam_init)

            q, k, v, u, g = _even_in(y_p, mod_p, nw, w_in, qw, kw, gsum_a, None, F32)
            new_a_k.append(k)
            new_a_v.append(v)
            y_p = _even_mix(q, k, v, None, u, g, y_p, mod_p, *pool_p, *shared)
            q, k, v, u, g = _even_in(y_s, mod_s, nw, w_in, qw, kw, gsum_a, rope_a, BF16)
            y_s = _even_mix(q, k, v, (cache_a_k, cache_a_v, j), u, g, y_s, mod_s, *pool_s, *shared)
        else:
            w_in = gqa_w_in[j].astype(BF16)
            w_out = gqa_w_out[j].astype(BF16)
            qw = gqa_q_norm_w[j].reshape(1, HD_C) * (HD_C ** -0.5)
            kw = gqa_k_norm_w[j].reshape(1, HD_C)

            q, k, v, g = _odd_in(y_p, mod_p, nw, w_in, qw, kw, gsum_c, None, F32)
            new_c_k.append(k)
            new_c_v.append(v)
            y_p = _odd_mix(q, k, v, None, g, y_p, mod_p, w_out)
            q, k, v, g = _odd_in(y_s, mod_s, nw, w_in, qw, kw, gsum_c, rope_c, BF16)
            y_s = _odd_mix(q, k, v, (cache_c_k, cache_c_v, j), g, y_s, mod_s, w_out)

    return (y_p, y_s, jnp.stack(new_a_k, axis=1), jnp.stack(new_a_v, axis=1),
            jnp.stack(new_c_k, axis=1), jnp.stack(new_c_v, axis=1))
```

```python
import functools
import math

import numpy as np
import jax
import jax.numpy as jnp
from jax import lax
from jax.experimental import pallas as pl
from jax.experimental.pallas import tpu as pltpu

F32 = jnp.float32
BF16 = jnp.bfloat16

GRID_W = 64
ROPE_THETA = 10000.0
EPS = 1e-6
H_A = 8
DK_A = 64
DV_A = 2 * DK_A
W_A = H_A * DV_A
POOL_WINDOWS = (2, 4, 8, 16)
N_POOL = len(POOL_WINDOWS)
HD_C = 128
KVH_C = 4
LOG2_E = math.log2(math.e)

LANES = 128
MXU_DIM = 256
VMEM_LIMIT_BYTES = 56 * 1024 * 1024

TOKEN_TILE = 256
ADA_TILE_N = 1024
SCORE_BYTES_PER_GROUP = 6 * 1024 * 1024


def _heads_per_group(n_heads, score_elems_per_head):
    per = n_heads
    while per > 1 and per * score_elems_per_head * 4 > SCORE_BYTES_PER_GROUP:
        per //= 2
    return per


def _compiler_params(n_axes):
    return pltpu.CompilerParams(
        dimension_semantics=("arbitrary",) * n_axes, vmem_limit_bytes=VMEM_LIMIT_BYTES)


def _resident(block_shape):
    zeros = (0,) * len(block_shape)
    return pl.BlockSpec(block_shape, lambda *_: zeros, pipeline_mode=pl.Buffered(1))


def _silu(x):
    return x / (1.0 + jnp.exp(-x))


def _exp2_rows(s):
    e = jnp.exp2(s - jnp.max(s, axis=-1, keepdims=True))
    return e, 1.0 / jnp.sum(e, axis=-1, keepdims=True)


def _dot(a, b):
    return jnp.dot(a, b, preferred_element_type=F32)


def _dot_nt(a, b):
    return lax.dot_general(a, b, (((1,), (1,)), ((), ())), preferred_element_type=F32)


def _ada_kernel(cond_ref, w_ref, b_ref, o_ref):
    s = _silu(cond_ref[...]).astype(BF16)
    o_ref[0] = _dot(s, w_ref[0].astype(BF16)) + b_ref[0]


def _ada_rows(cond, ada_w, ada_b):
    depth, d, n = ada_w.shape
    rows = cond.shape[0]
    return pl.pallas_call(
        _ada_kernel,
        grid=(depth, n // ADA_TILE_N),
        in_specs=[
            pl.BlockSpec((rows, d), lambda l, j: (0, 0)),
            pl.BlockSpec((1, d, ADA_TILE_N), lambda l, j: (l, 0, j)),
            pl.BlockSpec((1, 1, ADA_TILE_N), lambda l, j: (l, 0, j)),
        ],
        out_specs=pl.BlockSpec((1, rows, ADA_TILE_N), lambda l, j: (l, 0, j)),
        out_shape=jax.ShapeDtypeStruct((depth, rows, n), F32),
        compiler_params=_compiler_params(2),
        name="ada_rows",
    )(cond, ada_w, ada_b.reshape(depth, 1, n))


def _modulated_norm(x, mod_ref, nw_ref):
    r = lax.rsqrt(jnp.mean(x * x, axis=-1, keepdims=True) + EPS)
    a = nw_ref[...] * (1.0 + mod_ref[0, 1:2, :])
    return ((x * r) * a + mod_ref[0, 0:1, :]).astype(BF16)


def _group_rms_scale(t, gsum_ref, group):
    ms = _dot((t * t).astype(BF16), gsum_ref[...]) * (1.0 / group)
    return lax.rsqrt(ms + EPS)


def _rope(t, cos, sin_signed, quarter):
    lane = lax.broadcasted_iota(jnp.int32, t.shape, 1)
    upper = (lane & quarter) != 0
    rot = jnp.where(upper, pltpu.roll(t, quarter, 1), pltpu.roll(t, LANES - quarter, 1))
    return t * cos + rot * sin_signed


def _head_slabs(proj, w_row, gsum_ref, group, rope):
    slabs = []
    for c in range(proj.shape[1] // MXU_DIM):
        t = proj[:, c * MXU_DIM:(c + 1) * MXU_DIM]
        t = t * _group_rms_scale(t, gsum_ref, group)
        for j in range(MXU_DIM // LANES):
            slab = t[:, j * LANES:(j + 1) * LANES] * w_row
            if rope is not None:
                slab = _rope(slab, *rope)
            slabs.append(slab)
    return slabs


def _skewed_tiles(n_batch, tiles_per_batch):
    n_tiles = n_batch * tiles_per_batch

    def cur(s):
        c = jnp.minimum(s, n_tiles - 1)
        return c // tiles_per_batch, c % tiles_per_batch

    def prev(s):
        p = jnp.maximum(s - 1, 0)
        return p // tiles_per_batch, p % tiles_per_batch

    return n_tiles, cur, prev


def _start_projection(ymix_ref, ymix_prev_ref, wout_ref, x_ref, mod_ref, y_ref):
    @pl.when(pl.program_id(0) == 0)
    def _():
        ymix_ref[...] = jnp.zeros_like(ymix_ref)

    ymix_prev_ref[...] = ymix_ref[...]

    def project(lo, hi):
        out = _dot(ymix_prev_ref[...], wout_ref[:, lo:hi])
        y_ref[0, :, lo:hi] = x_ref[0, :, lo:hi] + mod_ref[0, 2:3, lo:hi] * out

    return project


def _even_in_kernel(*refs, rope):
    if rope:
        (x_ref, mod_ref, nw_ref, w_ref, qw_ref, kw_ref, gsum_ref, cos_ref, sin_ref,
         q_ref, k_ref, v_ref, u_ref, g_ref) = refs
        rope_args = (cos_ref[...], sin_ref[...], DK_A // 4)
    else:
        (x_ref, mod_ref, nw_ref, w_ref, qw_ref, kw_ref, gsum_ref,
         q_ref, k_ref, v_ref, u_ref, g_ref) = refs
        rope_args = None
    h = _modulated_norm(x_ref[0], mod_ref, nw_ref)
    w_b = u_ref.shape[2]

    q = _dot(h, w_ref[:, 0:W_A])
    for i, slab in enumerate(_head_slabs(q, qw_ref[...], gsum_ref, DK_A, rope_args)):
        q_ref[0, i] = slab.astype(q_ref.dtype)
    k = _dot(h, w_ref[:, W_A:2 * W_A])
    for i, slab in enumerate(_head_slabs(k, kw_ref[...], gsum_ref, DK_A, rope_args)):
        k_ref[0, i] = slab.astype(k_ref.dtype)
    v = _dot(h, w_ref[:, 2 * W_A:3 * W_A])
    for i in range(H_A):
        v_ref[0, i] = v[:, i * DV_A:(i + 1) * DV_A].astype(v_ref.dtype)
    u_ref[0] = _dot(h, w_ref[:, 3 * W_A:3 * W_A + w_b])
    g_ref[0] = _silu(_dot(h, w_ref[:, 3 * W_A + w_b:])).astype(g_ref.dtype)


def _even_in(x, mod, norm_w, w_in, qw, kw, gsum, rope, kv_dtype):
    b, l, d = x.shape
    n_in = w_in.shape[1]
    w_b = d - W_A
    tl = TOKEN_TILE
    tok = lambda i, t: (i, t, 0)
    head = lambda i, t: (i, 0, t, 0)
    in_specs = [
        pl.BlockSpec((1, tl, d), tok),
        pl.BlockSpec((1, 3, d), lambda i, t: (i if mod.shape[0] > 1 else 0, 0, 0)),
        _resident((1, d)),
        _resident((d, n_in)),
        _resident((1, LANES)),
        _resident((1, LANES)),
        _resident((MXU_DIM, MXU_DIM)),
    ]
    args = [x, mod, norm_w, w_in, qw, kw, gsum]
    if rope is not None:
        in_specs += [pl.BlockSpec((tl, LANES), lambda i, t: (t, 0))] * 2
        args += list(rope)
    head_shape = (b, H_A, l, DV_A)
    return pl.pallas_call(
        functools.partial(_even_in_kernel, rope=rope is not None),
        grid=(b, l // tl),
        in_specs=in_specs,
        out_specs=[
            pl.BlockSpec((1, H_A, tl, DV_A), head),
            pl.BlockSpec((1, H_A, tl, DV_A), head),
            pl.BlockSpec((1, H_A, tl, DV_A), head),
            pl.BlockSpec((1, tl, w_b), tok),
            pl.BlockSpec((1, tl, d), tok),
        ],
        out_shape=[
            jax.ShapeDtypeStruct(head_shape, BF16),
            jax.ShapeDtypeStruct(head_shape, kv_dtype),
            jax.ShapeDtypeStruct(head_shape, kv_dtype),
            jax.ShapeDtypeStruct((b, l, w_b), F32),
            jax.ShapeDtypeStruct((b, l, d), BF16),
        ],
        compiler_params=_compiler_params(2),
        name="even_in",
    )(*args)


def _even_mix_kernel(*refs, has_ctx, lam_init, n_tiles, tiles_per_batch):
    if has_ctx:
        (q_ref, k_ref, v_ref, ck_ref, cv_ref, u_ref, g_ref, x_ref, mod_ref, band_ref, invc_ref,
         lamv_ref, hmask_ref, subln_ref, poolw_ref, pscale_ref, wout_ref, y_ref,
         ymix_ref, ymix_prev_ref) = refs
    else:
        (q_ref, k_ref, v_ref, u_ref, g_ref, x_ref, mod_ref, band_ref, invc_ref,
         lamv_ref, hmask_ref, subln_ref, poolw_ref, pscale_ref, wout_ref, y_ref,
         ymix_ref, ymix_prev_ref) = refs
    tq = q_ref.shape[2]
    l = u_ref.shape[1]
    d = x_ref.shape[2]
    gw = poolw_ref.shape[1]
    project = _start_projection(ymix_ref, ymix_prev_ref, wout_ref, x_ref, mod_ref, y_ref)

    lv = lamv_ref[...]
    lam = (jnp.exp(jnp.sum(lv[0:1] * lv[1:2], axis=-1, keepdims=True))
           - jnp.exp(jnp.sum(lv[2:3] * lv[3:4], axis=-1, keepdims=True)) + lam_init)
    subln = subln_ref[...] * (1.0 - lam_init)

    if l == tq:
        row0 = 0
    else:
        tile = lax.rem(jnp.minimum(pl.program_id(0), n_tiles - 1), tiles_per_batch)
        row0 = pl.multiple_of(tile * tq, tq)
    for j in range(N_POOL):
        lo, hi = j * gw, (j + 1) * gw
        u_all = u_ref[0, :, lo:hi]
        u_own = u_all if l == tq else u_ref[0, pl.ds(row0, tq), lo:hi]
        dev = _dot(band_ref[j], u_all.astype(BF16)) * invc_ref[:, lo:hi] - u_own
        po = _dot(dev.astype(BF16), poolw_ref[j]) * pscale_ref[:, lo:hi]
        ymix_ref[:, W_A + lo:W_A + hi] = (
            po * g_ref[0, :, W_A + lo:W_A + hi].astype(F32)).astype(BF16)

    n_keys = k_ref.shape[2] + (ck_ref.shape[1] if has_ctx else 0)
    per_group = _heads_per_group(H_A, 2 * tq * n_keys)
    n_groups = H_A // per_group
    for gi in range(n_groups):
        heads = range(gi * per_group, (gi + 1) * per_group)
        scores, values = [], []
        for i in heads:
            qh = q_ref[0, i]
            kh = k_ref[0, i].astype(BF16)
            vh = v_ref[0, i].astype(BF16)
            if has_ctx:
                kh = jnp.concatenate([ck_ref[i].astype(BF16), kh], axis=0)
                vh = jnp.concatenate([cv_ref[i].astype(BF16), vh], axis=0)
            scores.append((_dot_nt(qh * hmask_ref[0:1, :], kh), _dot_nt(qh * hmask_ref[1:2, :], kh)))
            values.append(vh)
        project(gi * (d // n_groups), (gi + 1) * (d // n_groups))
        probs = []
        for s1, s2 in scores:
            e1, inv1 = _exp2_rows(s1)
            e2, inv2 = _exp2_rows(s2)
            probs.append((e1 * inv1 - e2 * (lam * inv2)).astype(BF16))
        outs = [_dot(p, vh) for p, vh in zip(probs, values)]
        for i, o in zip(heads, outs):
            lo, hi = i * DV_A, (i + 1) * DV_A
            o = o * lax.rsqrt(jnp.mean(o * o, axis=-1, keepdims=True) + EPS) * subln
            ymix_ref[:, lo:hi] = (o * g_ref[0, :, lo:hi].astype(F32)).astype(BF16)


def _even_mix(q, k, v, ctx, u, g, x, mod, band, invc, lamv, hmask, subln, poolw, pscale, w_out,
              lam_init):
    b, l, d = x.shape
    tq = TOKEN_TILE
    w_b = u.shape[2]
    gw = w_b // N_POOL
    n_tiles, cur, prev = _skewed_tiles(b, l // tq)
    cur_b = lambda s: cur(s)[0]
    in_specs = [
        pl.BlockSpec((1, H_A, tq, DV_A), lambda s: (cur(s)[0], 0, cur(s)[1], 0)),
        pl.BlockSpec((1, H_A, l, DV_A), lambda s: (cur_b(s), 0, 0, 0)),
        pl.BlockSpec((1, H_A, l, DV_A), lambda s: (cur_b(s), 0, 0, 0)),
    ]
    args = [q, k, v]
    if ctx is not None:
        ck, cv, layer = ctx
        p = ck.shape[3]
        in_specs += [pl.BlockSpec((None, None, H_A, p, DV_A), lambda s: (cur_b(s), layer, 0, 0, 0))] * 2
        args += [ck, cv]
    in_specs += [
        pl.BlockSpec((1, l, w_b), lambda s: (cur_b(s), 0, 0)),
        pl.BlockSpec((1, tq, d), lambda s: (*cur(s), 0)),
        pl.BlockSpec((1, tq, d), lambda s: (*prev(s), 0)),
        pl.BlockSpec((1, 3, d), lambda s: (prev(s)[0] if mod.shape[0] > 1 else 0, 0, 0)),
        pl.BlockSpec((N_POOL, tq, l), lambda s: (0, cur(s)[1], 0)),
        pl.BlockSpec((tq, w_b), lambda s: (cur(s)[1], 0)),
        _resident(lamv.shape),
        _resident(hmask.shape),
        _resident((1, DV_A)),
        _resident((N_POOL, gw, gw)),
        _resident((1, w_b)),
        _resident((d, d)),
    ]
    args += [u, g, x, mod, band, invc, lamv, hmask, subln, poolw, pscale, w_out]
    return pl.pallas_call(
        functools.partial(_even_mix_kernel, has_ctx=ctx is not None, lam_init=lam_init,
                          n_tiles=n_tiles, tiles_per_batch=l // tq),
        grid=(n_tiles + 1,),
        in_specs=in_specs,
        out_specs=pl.BlockSpec((1, tq, d), lambda s: (*prev(s), 0)),
        out_shape=jax.ShapeDtypeStruct((b, l, d), F32),
        scratch_shapes=[pltpu.VMEM((tq, d), BF16), pltpu.VMEM((tq, d), BF16)],
        compiler_params=_compiler_params(1),
        name="even_mix",
    )(*args)


def _odd_in_kernel(*refs, rope):
    if rope:
        (x_ref, mod_ref, nw_ref, w_ref, qw_ref, kw_ref, gsum_ref, cos_ref, sin_ref,
         q_ref, k_ref, v_ref, g_ref) = refs
        rope_args = (cos_ref[...], sin_ref[...], HD_C // 4)
    else:
        (x_ref, mod_ref, nw_ref, w_ref, qw_ref, kw_ref, gsum_ref,
         q_ref, k_ref, v_ref, g_ref) = refs
        rope_args = None
    h = _modulated_norm(x_ref[0], mod_ref, nw_ref)
    w_c = g_ref.shape[2]
    w_kv = KVH_C * HD_C

    q = _dot(h, w_ref[:, 0:w_c])
    for i, slab in enumerate(_head_slabs(q, qw_ref[...], gsum_ref, HD_C, rope_args)):
        q_ref[0, i] = slab.astype(q_ref.dtype)
    k = _dot(h, w_ref[:, w_c:w_c + w_kv])
    for i, slab in enumerate(_head_slabs(k, kw_ref[...], gsum_ref, HD_C, rope_args)):
        k_ref[0, i] = slab.astype(k_ref.dtype)
    v = _dot(h, w_ref[:, w_c + w_kv:w_c + 2 * w_kv])
    for i in range(KVH_C):
        v_ref[0, i] = v[:, i * HD_C:(i + 1) * HD_C].astype(v_ref.dtype)
    g_ref[0] = _silu(_dot(h, w_ref[:, w_c + 2 * w_kv:])).astype(g_ref.dtype)


def _odd_in(x, mod, norm_w, w_in, qw, kw, gsum, rope, kv_dtype):
    b, l, d = x.shape
    n_in = w_in.shape[1]
    h_c = d // HD_C
    tl = TOKEN_TILE
    tok = lambda i, t: (i, t, 0)
    head = lambda i, t: (i, 0, t, 0)
    in_specs = [
        pl.BlockSpec((1, tl, d), tok),
        pl.BlockSpec((1, 3, d), lambda i, t: (i if mod.shape[0] > 1 else 0, 0, 0)),
        _resident((1, d)),
        _resident((d, n_in)),
        _resident((1, LANES)),
        _resident((1, LANES)),
        _resident((MXU_DIM, MXU_DIM)),
    ]
    args = [x, mod, norm_w, w_in, qw, kw, gsum]
    if rope is not None:
        in_specs += [pl.BlockSpec((tl, LANES), lambda i, t: (t, 0))] * 2
        args += list(rope)
    return pl.pallas_call(
        functools.partial(_odd_in_kernel, rope=rope is not None),
        grid=(b, l // tl),
        in_specs=in_specs,
        out_specs=[
            pl.BlockSpec((1, h_c, tl, HD_C), head),
            pl.BlockSpec((1, KVH_C, tl, HD_C), head),
            pl.BlockSpec((1, KVH_C, tl, HD_C), head),
            pl.BlockSpec((1, tl, d), tok),
        ],
        out_shape=[
            jax.ShapeDtypeStruct((b, h_c, l, HD_C), BF16),
            jax.ShapeDtypeStruct((b, KVH_C, l, HD_C), kv_dtype),
            jax.ShapeDtypeStruct((b, KVH_C, l, HD_C), kv_dtype),
            jax.ShapeDtypeStruct((b, l, d), BF16),
        ],
        compiler_params=_compiler_params(2),
        name="odd_in",
    )(*args)


def _odd_mix_kernel(*refs, has_ctx):
    if has_ctx:
        (q_ref, k_ref, v_ref, ck_ref, cv_ref, g_ref, x_ref, mod_ref, wout_ref, y_ref,
         ymix_ref, ymix_prev_ref) = refs
    else:
        q_ref, k_ref, v_ref, g_ref, x_ref, mod_ref, wout_ref, y_ref, ymix_ref, ymix_prev_ref = refs
    h_c, tq = q_ref.shape[1], q_ref.shape[2]
    d = x_ref.shape[2]
    rep = h_c // KVH_C
    project = _start_projection(ymix_ref, ymix_prev_ref, wout_ref, x_ref, mod_ref, y_ref)

    n_keys = k_ref.shape[2] + (ck_ref.shape[1] if has_ctx else 0)
    per_group = _heads_per_group(KVH_C, rep * tq * n_keys)
    n_groups = KVH_C // per_group
    for gi in range(n_groups):
        kv_heads = range(gi * per_group, (gi + 1) * per_group)
        scores, values = [], []
        for j in kv_heads:
            kh = k_ref[0, j].astype(BF16)
            vh = v_ref[0, j].astype(BF16)
            if has_ctx:
                kh = jnp.concatenate([ck_ref[j].astype(BF16), kh], axis=0)
                vh = jnp.concatenate([cv_ref[j].astype(BF16), vh], axis=0)
            qs = q_ref[0, j * rep:(j + 1) * rep].reshape(rep * tq, HD_C)
            scores.append(_dot_nt(qs, kh))
            values.append(vh)
        project(gi * (d // n_groups), (gi + 1) * (d // n_groups))
        exps = [_exp2_rows(s) for s in scores]
        outs = [_dot(e.astype(BF16), vh) * inv for (e, inv), vh in zip(exps, values)]
        for j, o in zip(kv_heads, outs):
            for r in range(rep):
                lo, hi = (j * rep + r) * HD_C, (j * rep + r + 1) * HD_C
                ymix_ref[:, lo:hi] = (
                    o[r * tq:(r + 1) * tq] * g_ref[0, :, lo:hi].astype(F32)).astype(BF16)


def _odd_mix(q, k, v, ctx, g, x, mod, w_out):
    b, l, d = x.shape
    h_c = q.shape[1]
    tq = TOKEN_TILE
    n_tiles, cur, prev = _skewed_tiles(b, l // tq)
    cur_b = lambda s: cur(s)[0]
    in_specs = [
        pl.BlockSpec((1, h_c, tq, HD_C), lambda s: (cur(s)[0], 0, cur(s)[1], 0)),
        pl.BlockSpec((1, KVH_C, l, HD_C), lambda s: (cur_b(s), 0, 0, 0)),
        pl.BlockSpec((1, KVH_C, l, HD_C), lambda s: (cur_b(s), 0, 0, 0)),
    ]
    args = [q, k, v]
    if ctx is not None:
        ck, cv, layer = ctx
        p = ck.shape[3]
        in_specs += [pl.BlockSpec((None, None, KVH_C, p, HD_C), lambda s: (cur_b(s), layer, 0, 0, 0))] * 2
        args += [ck, cv]
    in_specs += [
        pl.BlockSpec((1, tq, d), lambda s: (*cur(s), 0)),
        pl.BlockSpec((1, tq, d), lambda s: (*prev(s), 0)),
        pl.BlockSpec((1, 3, d), lambda s: (prev(s)[0] if mod.shape[0] > 1 else 0, 0, 0)),
        _resident((d, d)),
    ]
    args += [g, x, mod, w_out]
    return pl.pallas_call(
        functools.partial(_odd_mix_kernel, has_ctx=ctx is not None),
        grid=(n_tiles + 1,),
        in_specs=in_specs,
        out_specs=pl.BlockSpec((1, tq, d), lambda s: (*prev(s), 0)),
        out_shape=jax.ShapeDtypeStruct((b, l, d), F32),
        scratch_shapes=[pltpu.VMEM((tq, d), BF16), pltpu.VMEM((tq, d), BF16)],
        compiler_params=_compiler_params(1),
        name="odd_mix",
    )(*args)


def _rope_tables(n_tokens, dim):
    rows = n_tokens // GRID_W
    row = jnp.repeat(jnp.arange(rows), GRID_W).astype(F32)
    col = jnp.tile(jnp.arange(GRID_W), rows).astype(F32)
    quarter = dim // 4
    freqs = ROPE_THETA ** (-jnp.arange(quarter, dtype=F32) / quarter)
    ar = row[:, None] * freqs
    ac = col[:, None] * freqs
    cos = jnp.concatenate([jnp.cos(ar), jnp.cos(ar), jnp.cos(ac), jnp.cos(ac)], axis=-1)
    sin = jnp.concatenate([-jnp.sin(ar), jnp.sin(ar), -jnp.sin(ac), jnp.sin(ac)], axis=-1)
    reps = LANES // dim
    return jnp.tile(cos, (1, reps)), jnp.tile(sin, (1, reps))


def _group_sum_matrix(group):
    idx = np.arange(MXU_DIM) // group
    return jnp.asarray(idx[:, None] == idx[None, :], dtype=BF16)


def _pool_tables(l, gw):
    t = np.arange(l)[:, None]
    s = np.arange(l)[None, :]
    bands, invs = [], []
    for w in POOL_WINDOWS:
        lo = np.clip(t - w // 2, 0, l)
        hi = np.clip(t + w // 2, 0, l)
        bands.append((s >= lo) & (s < hi))
        invs.append(np.repeat(1.0 / (hi - lo).astype(np.float32), gw, axis=1))
    return (jnp.asarray(np.stack(bands), dtype=BF16),
            jnp.asarray(np.concatenate(invs, axis=1), dtype=F32))


def kernel(x_prompt, x_sample, cache_a_k, cache_a_v, cache_c_k, cache_c_v, c, c_ctx, norm_w, ada_w, ada_b, even_w_in, even_q_norm_w, even_k_norm_w, even_lam_q1, even_lam_k1, even_lam_q2, even_lam_k2, even_subln_w, even_pool_w, even_pool_scale, even_w_out, gqa_w_in, gqa_q_norm_w, gqa_k_norm_w, gqa_w_out):
    depth, d = norm_w.shape
    n_lat = x_sample.shape[1]
    n_dec = x_sample.shape[0]
    w_b = d - W_A
    gw = w_b // N_POOL

    rows = -(-(1 + n_dec) // 8) * 8
    cond = jnp.zeros((rows, d), F32).at[0].set(c_ctx).at[1:1 + n_dec].set(c)
    mod = _ada_rows(cond, ada_w, ada_b).reshape(depth, rows, 3, d)

    rope_a = _rope_tables(n_lat, DK_A)
    rope_c = _rope_tables(n_lat, HD_C)
    gsum_a = _group_sum_matrix(DK_A)
    gsum_c = _group_sum_matrix(HD_C)
    pool_p = _pool_tables(x_prompt.shape[1], gw)
    pool_s = _pool_tables(n_lat, gw)
    hmask = jnp.asarray(np.arange(LANES)[None, :] // DK_A == np.arange(2)[:, None], dtype=BF16)

    y_p, y_s = x_prompt, x_sample
    new_a_k, new_a_v, new_c_k, new_c_v = [], [], [], []
    for i in range(depth):
        j = i // 2
        mod_p = mod[i, 0:1]
        mod_s = mod[i, 1:1 + n_dec]
        nw = norm_w[i].reshape(1, d)
        if i % 2 == 0:
            lam_init = 0.8 - 0.6 * math.exp(-0.3 * i)
            w_in = even_w_in[j].astype(BF16)
            w_out = even_w_out[j].astype(BF16)
            poolw = even_pool_w[j].astype(BF16)
            qw = jnp.tile(even_q_norm_w[j], LANES // DK_A).reshape(1, LANES) * (DK_A ** -0.5 * LOG2_E)
            kw = jnp.tile(even_k_norm_w[j], LANES // DK_A).reshape(1, LANES)
            lamv = jnp.stack([even_lam_q1[j], even_lam_k1[j], even_lam_q2[j], even_lam_k2[j]])
            subln = even_subln_w[j].reshape(1, DV_A)
            pscale = even_pool_scale[j].reshape(1, w_b)
            shared = (lamv, hmask, subln, poolw, pscale, w_out, lam_init)

            q, k, v, u, g = _even_in(y_p, mod_p, nw, w_in, qw, kw, gsum_a, None, F32)
            new_a_k.append(k)
            new_a_v.append(v)
            y_p = _even_mix(q, k, v, None, u, g, y_p, mod_p, *pool_p, *shared)
            q, k, v, u, g = _even_in(y_s, mod_s, nw, w_in, qw, kw, gsum_a, rope_a, BF16)
            y_s = _even_mix(q, k, v, (cache_a_k, cache_a_v, j), u, g, y_s, mod_s, *pool_s, *shared)
        else:
            w_in = gqa_w_in[j].astype(BF16)
            w_out = gqa_w_out[j].astype(BF16)
            qw = gqa_q_norm_w[j].reshape(1, HD_C) * (HD_C ** -0.5 * LOG2_E)
            kw = gqa_k_norm_w[j].reshape(1, HD_C)

            q, k, v, g = _odd_in(y_p, mod_p, nw, w_in, qw, kw, gsum_c, None, F32)
            new_c_k.append(k)
            new_c_v.append(v)
            y_p = _odd_mix(q, k, v, None, g, y_p, mod_p, w_out)
            q, k, v, g = _odd_in(y_s, mod_s, nw, w_in, qw, kw, gsum_c, rope_c, BF16)
            y_s = _odd_mix(q, k, v, (cache_c_k, cache_c_v, j), g, y_s, mod_s, w_out)

    return (y_p, y_s, jnp.stack(new_a_k, axis=1), jnp.stack(new_a_v, axis=1),
            jnp.stack(new_c_k, axis=1), jnp.stack(new_c_v, axis=1))
```

```python
import functools
import math

import numpy as np
import jax
import jax.numpy as jnp
from jax import lax
from jax.experimental import pallas as pl
from jax.experimental.pallas import tpu as pltpu

F32 = jnp.float32
BF16 = jnp.bfloat16

GRID_W = 64
ROPE_THETA = 10000.0
EPS = 1e-6
H_A = 8
DK_A = 64
DV_A = 2 * DK_A
W_A = H_A * DV_A
POOL_WINDOWS = (2, 4, 8, 16)
N_POOL = len(POOL_WINDOWS)
HD_C = 128
KVH_C = 4
LOG2_E = math.log2(math.e)

LANES = 128
MXU_DIM = 256
VMEM_LIMIT_BYTES = 56 * 1024 * 1024

TOKEN_TILE = 256
ADA_TILE_N = 2048
SCORE_BYTES_PER_GROUP = 6 * 1024 * 1024


def _heads_per_group(n_heads, score_elems_per_head):
    per = n_heads
    while per > 1 and per * score_elems_per_head * 4 > SCORE_BYTES_PER_GROUP:
        per //= 2
    return per


def _compiler_params(n_axes):
    return pltpu.CompilerParams(
        dimension_semantics=("arbitrary",) * n_axes, vmem_limit_bytes=VMEM_LIMIT_BYTES)


def _resident(block_shape):
    zeros = (0,) * len(block_shape)
    return pl.BlockSpec(block_shape, lambda *_: zeros, pipeline_mode=pl.Buffered(1))


def _silu(x):
    return x / (1.0 + jnp.exp(-x))


def _exp2_rows(s):
    e = jnp.exp2(s - jnp.max(s, axis=-1, keepdims=True))
    return e, 1.0 / jnp.sum(e, axis=-1, keepdims=True)


def _dot(a, b):
    return jnp.dot(a, b, preferred_element_type=F32)


def _dot_nt(a, b):
    return lax.dot_general(a, b, (((1,), (1,)), ((), ())), preferred_element_type=F32)


def _ada_kernel(cond_ref, w_ref, b_ref, o_ref):
    s = _silu(cond_ref[...]).astype(BF16)
    o_ref[0] = _dot(s, w_ref[0].astype(BF16)) + b_ref[0]


def _ada_rows(cond, ada_w, ada_b):
    depth, d, n = ada_w.shape
    rows = cond.shape[0]
    return pl.pallas_call(
        _ada_kernel,
        grid=(depth, n // ADA_TILE_N),
        in_specs=[
            pl.BlockSpec((rows, d), lambda l, j: (0, 0)),
            pl.BlockSpec((1, d, ADA_TILE_N), lambda l, j: (l, 0, j)),
            pl.BlockSpec((1, 1, ADA_TILE_N), lambda l, j: (l, 0, j)),
        ],
        out_specs=pl.BlockSpec((1, rows, ADA_TILE_N), lambda l, j: (l, 0, j)),
        out_shape=jax.ShapeDtypeStruct((depth, rows, n), F32),
        compiler_params=_compiler_params(2),
        name="ada_rows",
    )(cond, ada_w, ada_b.reshape(depth, 1, n))


def _modulated_norm(x, mod_ref, nw_ref):
    r = lax.rsqrt(jnp.mean(x * x, axis=-1, keepdims=True) + EPS)
    a = nw_ref[...] * (1.0 + mod_ref[0, 1:2, :])
    return ((x * r) * a + mod_ref[0, 0:1, :]).astype(BF16)


def _group_rms_scale(t, group):
    sq = t * t
    if group == LANES:
        return lax.rsqrt(jnp.mean(sq, axis=-1, keepdims=True) + EPS)
    assert 2 * group == LANES
    upper = lax.broadcasted_iota(jnp.int32, t.shape, 1) >= group
    lo = jnp.sum(jnp.where(upper, 0.0, sq), axis=-1, keepdims=True)
    hi = jnp.sum(jnp.where(upper, sq, 0.0), axis=-1, keepdims=True)
    return jnp.where(upper, lax.rsqrt(hi * (1.0 / group) + EPS), lax.rsqrt(lo * (1.0 / group) + EPS))


def _rope(t, cos, sin_signed, quarter):
    lane = lax.broadcasted_iota(jnp.int32, t.shape, 1)
    upper = (lane & quarter) != 0
    rot = jnp.where(upper, pltpu.roll(t, quarter, 1), pltpu.roll(t, LANES - quarter, 1))
    return t * cos + rot * sin_signed


def _head_slabs(proj, w_row, group, rope):
    slabs = []
    for j in range(proj.shape[1] // LANES):
        t = proj[:, j * LANES:(j + 1) * LANES]
        slab = t * _group_rms_scale(t, group) * w_row
        if rope is not None:
            slab = _rope(slab, *rope)
        slabs.append(slab)
    return slabs


def _skewed_tiles(n_batch, tiles_per_batch):
    n_tiles = n_batch * tiles_per_batch

    def cur(s):
        c = jnp.minimum(s, n_tiles - 1)
        return c // tiles_per_batch, c % tiles_per_batch

    def prev(s):
        p = jnp.maximum(s - 1, 0)
        return p // tiles_per_batch, p % tiles_per_batch

    return n_tiles, cur, prev


def _start_projection(ymix_ref, ymix_prev_ref, wout_ref, x_ref, mod_ref, y_ref):
    @pl.when(pl.program_id(0) == 0)
    def _():
        ymix_ref[...] = jnp.zeros_like(ymix_ref)

    ymix_prev_ref[...] = ymix_ref[...]

    def project(lo, hi):
        out = _dot(ymix_prev_ref[...], wout_ref[:, lo:hi])
        y_ref[0, :, lo:hi] = x_ref[0, :, lo:hi] + mod_ref[0, 2:3, lo:hi] * out

    return project


def _even_in_kernel(*refs, rope):
    if rope:
        (x_ref, mod_ref, nw_ref, w_ref, qw_ref, kw_ref, cos_ref, sin_ref,
         q_ref, k_ref, v_ref, u_ref, g_ref) = refs
        rope_args = (cos_ref[...], sin_ref[...], DK_A // 4)
    else:
        (x_ref, mod_ref, nw_ref, w_ref, qw_ref, kw_ref,
         q_ref, k_ref, v_ref, u_ref, g_ref) = refs
        rope_args = None
    h = _modulated_norm(x_ref[0], mod_ref, nw_ref)
    w_b = u_ref.shape[2]

    q = _dot(h, w_ref[:, 0:W_A])
    for i, slab in enumerate(_head_slabs(q, qw_ref[...], DK_A, rope_args)):
        q_ref[0, i] = slab.astype(q_ref.dtype)
    k = _dot(h, w_ref[:, W_A:2 * W_A])
    for i, slab in enumerate(_head_slabs(k, kw_ref[...], DK_A, rope_args)):
        k_ref[0, i] = slab.astype(k_ref.dtype)
    v = _dot(h, w_ref[:, 2 * W_A:3 * W_A])
    for i in range(H_A):
        v_ref[0, i] = v[:, i * DV_A:(i + 1) * DV_A].astype(v_ref.dtype)
    u_ref[0] = _dot(h, w_ref[:, 3 * W_A:3 * W_A + w_b])
    g_ref[0] = _silu(_dot(h, w_ref[:, 3 * W_A + w_b:])).astype(g_ref.dtype)


def _even_in(x, mod, norm_w, w_in, qw, kw, rope, kv_dtype):
    b, l, d = x.shape
    n_in = w_in.shape[1]
    w_b = d - W_A
    tl = TOKEN_TILE
    tok = lambda i, t: (i, t, 0)
    head = lambda i, t: (i, 0, t, 0)
    in_specs = [
        pl.BlockSpec((1, tl, d), tok),
        pl.BlockSpec((1, 3, d), lambda i, t: (i if mod.shape[0] > 1 else 0, 0, 0)),
        _resident((1, d)),
        _resident((d, n_in)),
        _resident((1, LANES)),
        _resident((1, LANES)),
    ]
    args = [x, mod, norm_w, w_in, qw, kw]
    if rope is not None:
        in_specs += [pl.BlockSpec((tl, LANES), lambda i, t: (t, 0))] * 2
        args += list(rope)
    head_shape = (b, H_A, l, DV_A)
    return pl.pallas_call(
        functools.partial(_even_in_kernel, rope=rope is not None),
        grid=(b, l // tl),
        in_specs=in_specs,
        out_specs=[
            pl.BlockSpec((1, H_A, tl, DV_A), head),
            pl.BlockSpec((1, H_A, tl, DV_A), head),
            pl.BlockSpec((1, H_A, tl, DV_A), head),
            pl.BlockSpec((1, tl, w_b), tok),
            pl.BlockSpec((1, tl, d), tok),
        ],
        out_shape=[
            jax.ShapeDtypeStruct(head_shape, BF16),
            jax.ShapeDtypeStruct(head_shape, kv_dtype),
            jax.ShapeDtypeStruct(head_shape, kv_dtype),
            jax.ShapeDtypeStruct((b, l, w_b), F32),
            jax.ShapeDtypeStruct((b, l, d), BF16),
        ],
        compiler_params=_compiler_params(2),
        name="even_in",
    )(*args)


def _even_mix_kernel(*refs, has_ctx, lam_init, n_tiles, tiles_per_batch):
    if has_ctx:
        (q_ref, k_ref, v_ref, ck_ref, cv_ref, u_ref, g_ref, x_ref, mod_ref, band_ref, invc_ref,
         lamv_ref, hmask_ref, subln_ref, poolw_ref, pscale_ref, wout_ref, y_ref,
         ymix_ref, ymix_prev_ref) = refs
    else:
        (q_ref, k_ref, v_ref, u_ref, g_ref, x_ref, mod_ref, band_ref, invc_ref,
         lamv_ref, hmask_ref, subln_ref, poolw_ref, pscale_ref, wout_ref, y_ref,
         ymix_ref, ymix_prev_ref) = refs
    tq = q_ref.shape[2]
    l = u_ref.shape[1]
    d = x_ref.shape[2]
    gw = poolw_ref.shape[1]
    project = _start_projection(ymix_ref, ymix_prev_ref, wout_ref, x_ref, mod_ref, y_ref)

    lv = lamv_ref[...]
    lam = (jnp.exp(jnp.sum(lv[0:1] * lv[1:2], axis=-1, keepdims=True))
           - jnp.exp(jnp.sum(lv[2:3] * lv[3:4], axis=-1, keepdims=True)) + lam_init)
    subln = subln_ref[...] * (1.0 - lam_init)

    if l == tq:
        row0 = 0
    else:
        tile = lax.rem(jnp.minimum(pl.program_id(0), n_tiles - 1), tiles_per_batch)
        row0 = pl.multiple_of(tile * tq, tq)
    for j in range(N_POOL):
        lo, hi = j * gw, (j + 1) * gw
        u_all = u_ref[0, :, lo:hi]
        u_own = u_all if l == tq else u_ref[0, pl.ds(row0, tq), lo:hi]
        dev = _dot(band_ref[j], u_all.astype(BF16)) * invc_ref[:, lo:hi] - u_own
        po = _dot(dev.astype(BF16), poolw_ref[j]) * pscale_ref[:, lo:hi]
        ymix_ref[:, W_A + lo:W_A + hi] = (
            po * g_ref[0, :, W_A + lo:W_A + hi].astype(F32)).astype(BF16)

    n_keys = k_ref.shape[2] + (ck_ref.shape[1] if has_ctx else 0)
    per_group = _heads_per_group(H_A, 2 * tq * n_keys)
    n_groups = H_A // per_group
    for gi in range(n_groups):
        heads = range(gi * per_group, (gi + 1) * per_group)
        scores, values = [], []
        for i in heads:
            qh = q_ref[0, i]
            kh = k_ref[0, i].astype(BF16)
            vh = v_ref[0, i].astype(BF16)
            if has_ctx:
                kh = jnp.concatenate([ck_ref[i].astype(BF16), kh], axis=0)
                vh = jnp.concatenate([cv_ref[i].astype(BF16), vh], axis=0)
            scores.append((_dot_nt(qh * hmask_ref[0:1, :], kh), _dot_nt(qh * hmask_ref[1:2, :], kh)))
            values.append(vh)
        project(gi * (d // n_groups), (gi + 1) * (d // n_groups))
        probs = []
        for s1, s2 in scores:
            e1, inv1 = _exp2_rows(s1)
            e2, inv2 = _exp2_rows(s2)
            probs.append((e1 * inv1 - e2 * (lam * inv2)).astype(BF16))
        outs = [_dot(p, vh) for p, vh in zip(probs, values)]
        for i, o in zip(heads, outs):
            lo, hi = i * DV_A, (i + 1) * DV_A
            o = o * lax.rsqrt(jnp.mean(o * o, axis=-1, keepdims=True) + EPS) * subln
            ymix_ref[:, lo:hi] = (o * g_ref[0, :, lo:hi].astype(F32)).astype(BF16)


def _even_mix(q, k, v, ctx, u, g, x, mod, band, invc, lamv, hmask, subln, poolw, pscale, w_out,
              lam_init):
    b, l, d = x.shape
    tq = TOKEN_TILE
    w_b = u.shape[2]
    gw = w_b // N_POOL
    n_tiles, cur, prev = _skewed_tiles(b, l // tq)
    cur_b = lambda s: cur(s)[0]
    in_specs = [
        pl.BlockSpec((1, H_A, tq, DV_A), lambda s: (cur(s)[0], 0, cur(s)[1], 0)),
        pl.BlockSpec((1, H_A, l, DV_A), lambda s: (cur_b(s), 0, 0, 0)),
        pl.BlockSpec((1, H_A, l, DV_A), lambda s: (cur_b(s), 0, 0, 0)),
    ]
    args = [q, k, v]
    if ctx is not None:
        ck, cv, layer = ctx
        p = ck.shape[3]
        in_specs += [pl.BlockSpec((None, None, H_A, p, DV_A), lambda s: (cur_b(s), layer, 0, 0, 0))] * 2
        args += [ck, cv]
    in_specs += [
        pl.BlockSpec((1, l, w_b), lambda s: (cur_b(s), 0, 0)),
        pl.BlockSpec((1, tq, d), lambda s: (*cur(s), 0)),
        pl.BlockSpec((1, tq, d), lambda s: (*prev(s), 0)),
        pl.BlockSpec((1, 3, d), lambda s: (prev(s)[0] if mod.shape[0] > 1 else 0, 0, 0)),
        pl.BlockSpec((N_POOL, tq, l), lambda s: (0, cur(s)[1], 0)),
        pl.BlockSpec((tq, w_b), lambda s: (cur(s)[1], 0)),
        _resident(lamv.shape),
        _resident(hmask.shape),
        _resident((1, DV_A)),
        _resident((N_POOL, gw, gw)),
        _resident((1, w_b)),
        _resident((d, d)),
    ]
    args += [u, g, x, mod, band, invc, lamv, hmask, subln, poolw, pscale, w_out]
    return pl.pallas_call(
        functools.partial(_even_mix_kernel, has_ctx=ctx is not None, lam_init=lam_init,
                          n_tiles=n_tiles, tiles_per_batch=l // tq),
        grid=(n_tiles + 1,),
        in_specs=in_specs,
        out_specs=pl.BlockSpec((1, tq, d), lambda s: (*prev(s), 0)),
        out_shape=jax.ShapeDtypeStruct((b, l, d), F32),
        scratch_shapes=[pltpu.VMEM((tq, d), BF16), pltpu.VMEM((tq, d), BF16)],
        compiler_params=_compiler_params(1),
        name="even_mix",
    )(*args)


def _odd_in_kernel(*refs, rope):
    if rope:
        (x_ref, mod_ref, nw_ref, w_ref, qw_ref, kw_ref, cos_ref, sin_ref,
         q_ref, k_ref, v_ref, g_ref) = refs
        rope_args = (cos_ref[...], sin_ref[...], HD_C // 4)
    else:
        (x_ref, mod_ref, nw_ref, w_ref, qw_ref, kw_ref,
         q_ref, k_ref, v_ref, g_ref) = refs
        rope_args = None
    h = _modulated_norm(x_ref[0], mod_ref, nw_ref)
    w_c = g_ref.shape[2]
    w_kv = KVH_C * HD_C

    q = _dot(h, w_ref[:, 0:w_c])
    for i, slab in enumerate(_head_slabs(q, qw_ref[...], HD_C, rope_args)):
        q_ref[0, i] = slab.astype(q_ref.dtype)
    k = _dot(h, w_ref[:, w_c:w_c + w_kv])
    for i, slab in enumerate(_head_slabs(k, kw_ref[...], HD_C, rope_args)):
        k_ref[0, i] = slab.astype(k_ref.dtype)
    v = _dot(h, w_ref[:, w_c + w_kv:w_c + 2 * w_kv])
    for i in range(KVH_C):
        v_ref[0, i] = v[:, i * HD_C:(i + 1) * HD_C].astype(v_ref.dtype)
    g_ref[0] = _silu(_dot(h, w_ref[:, w_c + 2 * w_kv:])).astype(g_ref.dtype)


def _odd_in(x, mod, norm_w, w_in, qw, kw, rope, kv_dtype):
    b, l, d = x.shape
    n_in = w_in.shape[1]
    h_c = d // HD_C
    tl = TOKEN_TILE
    tok = lambda i, t: (i, t, 0)
    head = lambda i, t: (i, 0, t, 0)
    in_specs = [
        pl.BlockSpec((1, tl, d), tok),
        pl.BlockSpec((1, 3, d), lambda i, t: (i if mod.shape[0] > 1 else 0, 0, 0)),
        _resident((1, d)),
        _resident((d, n_in)),
        _resident((1, LANES)),
        _resident((1, LANES)),
    ]
    args = [x, mod, norm_w, w_in, qw, kw]
    if rope is not None:
        in_specs += [pl.BlockSpec((tl, LANES), lambda i, t: (t, 0))] * 2
        args += list(rope)
    return pl.pallas_call(
        functools.partial(_odd_in_kernel, rope=rope is not None),
        grid=(b, l // tl),
        in_specs=in_specs,
        out_specs=[
            pl.BlockSpec((1, h_c, tl, HD_C), head),
            pl.BlockSpec((1, KVH_C, tl, HD_C), head),
            pl.BlockSpec((1, KVH_C, tl, HD_C), head),
            pl.BlockSpec((1, tl, d), tok),
        ],
        out_shape=[
            jax.ShapeDtypeStruct((b, h_c, l, HD_C), BF16),
            jax.ShapeDtypeStruct((b, KVH_C, l, HD_C), kv_dtype),
            jax.ShapeDtypeStruct((b, KVH_C, l, HD_C), kv_dtype),
            jax.ShapeDtypeStruct((b, l, d), BF16),
        ],
        compiler_params=_compiler_params(2),
        name="odd_in",
    )(*args)


def _odd_mix_kernel(*refs, has_ctx):
    if has_ctx:
        (q_ref, k_ref, v_ref, ck_ref, cv_ref, g_ref, x_ref, mod_ref, wout_ref, y_ref,
         ymix_ref, ymix_prev_ref) = refs
    else:
        q_ref, k_ref, v_ref, g_ref, x_ref, mod_ref, wout_ref, y_ref, ymix_ref, ymix_prev_ref = refs
    h_c, tq = q_ref.shape[1], q_ref.shape[2]
    d = x_ref.shape[2]
    rep = h_c // KVH_C
    project = _start_projection(ymix_ref, ymix_prev_ref, wout_ref, x_ref, mod_ref, y_ref)

    n_keys = k_ref.shape[2] + (ck_ref.shape[1] if has_ctx else 0)
    per_group = _heads_per_group(KVH_C, rep * tq * n_keys)
    n_groups = KVH_C // per_group
    for gi in range(n_groups):
        kv_heads = range(gi * per_group, (gi + 1) * per_group)
        scores, values = [], []
        for j in kv_heads:
            kh = k_ref[0, j].astype(BF16)
            vh = v_ref[0, j].astype(BF16)
            if has_ctx:
                kh = jnp.concatenate([ck_ref[j].astype(BF16), kh], axis=0)
                vh = jnp.concatenate([cv_ref[j].astype(BF16), vh], axis=0)
            qs = q_ref[0, j * rep:(j + 1) * rep].reshape(rep * tq, HD_C)
            scores.append(_dot_nt(qs, kh))
            values.append(vh)
        project(gi * (d // n_groups), (gi + 1) * (d // n_groups))
        exps = [_exp2_rows(s) for s in scores]
        outs = [_dot(e.astype(BF16), vh) * inv for (e, inv), vh in zip(exps, values)]
        for j, o in zip(kv_heads, outs):
            for r in range(rep):
                lo, hi = (j * rep + r) * HD_C, (j * rep + r + 1) * HD_C
                ymix_ref[:, lo:hi] = (
                    o[r * tq:(r + 1) * tq] * g_ref[0, :, lo:hi].astype(F32)).astype(BF16)


def _odd_mix(q, k, v, ctx, g, x, mod, w_out):
    b, l, d = x.shape
    h_c = q.shape[1]
    tq = TOKEN_TILE
    n_tiles, cur, prev = _skewed_tiles(b, l // tq)
    cur_b = lambda s: cur(s)[0]
    in_specs = [
        pl.BlockSpec((1, h_c, tq, HD_C), lambda s: (cur(s)[0], 0, cur(s)[1], 0)),
        pl.BlockSpec((1, KVH_C, l, HD_C), lambda s: (cur_b(s), 0, 0, 0)),
        pl.BlockSpec((1, KVH_C, l, HD_C), lambda s: (cur_b(s), 0, 0, 0)),
    ]
    args = [q, k, v]
    if ctx is not None:
        ck, cv, layer = ctx
        p = ck.shape[3]
        in_specs += [pl.BlockSpec((None, None, KVH_C, p, HD_C), lambda s: (cur_b(s), layer, 0, 0, 0))] * 2
        args += [ck, cv]
    in_specs += [
        pl.BlockSpec((1, tq, d), lambda s: (*cur(s), 0)),
        pl.BlockSpec((1, tq, d), lambda s: (*prev(s), 0)),
        pl.BlockSpec((1, 3, d), lambda s: (prev(s)[0] if mod.shape[0] > 1 else 0, 0, 0)),
        _resident((d, d)),
    ]
    args += [g, x, mod, w_out]
    return pl.pallas_call(
        functools.partial(_odd_mix_kernel, has_ctx=ctx is not None),
        grid=(n_tiles + 1,),
        in_specs=in_specs,
        out_specs=pl.BlockSpec((1, tq, d), lambda s: (*prev(s), 0)),
        out_shape=jax.ShapeDtypeStruct((b, l, d), F32),
        scratch_shapes=[pltpu.VMEM((tq, d), BF16), pltpu.VMEM((tq, d), BF16)],
        compiler_params=_compiler_params(1),
        name="odd_mix",
    )(*args)


def _rope_tables(n_tokens, dim):
    rows = n_tokens // GRID_W
    row = jnp.repeat(jnp.arange(rows), GRID_W).astype(F32)
    col = jnp.tile(jnp.arange(GRID_W), rows).astype(F32)
    quarter = dim // 4
    freqs = ROPE_THETA ** (-jnp.arange(quarter, dtype=F32) / quarter)
    ar = row[:, None] * freqs
    ac = col[:, None] * freqs
    cos = jnp.concatenate([jnp.cos(ar), jnp.cos(ar), jnp.cos(ac), jnp.cos(ac)], axis=-1)
    sin = jnp.concatenate([-jnp.sin(ar), jnp.sin(ar), -jnp.sin(ac), jnp.sin(ac)], axis=-1)
    reps = LANES // dim
    return jnp.tile(cos, (1, reps)), jnp.tile(sin, (1, reps))


def _pool_tables(l, gw):
    t = np.arange(l)[:, None]
    s = np.arange(l)[None, :]
    bands, invs = [], []
    for w in POOL_WINDOWS:
        lo = np.clip(t - w // 2, 0, l)
        hi = np.clip(t + w // 2, 0, l)
        bands.append((s >= lo) & (s < hi))
        invs.append(np.repeat(1.0 / (hi - lo).astype(np.float32), gw, axis=1))
    return (jnp.asarray(np.stack(bands), dtype=BF16),
            jnp.asarray(np.concatenate(invs, axis=1), dtype=F32))


def kernel(x_prompt, x_sample, cache_a_k, cache_a_v, cache_c_k, cache_c_v, c, c_ctx, norm_w, ada_w, ada_b, even_w_in, even_q_norm_w, even_k_norm_w, even_lam_q1, even_lam_k1, even_lam_q2, even_lam_k2, even_subln_w, even_pool_w, even_pool_scale, even_w_out, gqa_w_in, gqa_q_norm_w, gqa_k_norm_w, gqa_w_out):
    depth, d = norm_w.shape
    n_lat = x_sample.shape[1]
    n_dec = x_sample.shape[0]
    w_b = d - W_A
    gw = w_b // N_POOL

    rows = -(-(1 + n_dec) // 8) * 8
    cond = jnp.zeros((rows, d), F32).at[0].set(c_ctx).at[1:1 + n_dec].set(c)
    mod = _ada_rows(cond, ada_w, ada_b).reshape(depth, rows, 3, d)

    rope_a = _rope_tables(n_lat, DK_A)
    rope_c = _rope_tables(n_lat, HD_C)
    pool_p = _pool_tables(x_prompt.shape[1], gw)
    pool_s = _pool_tables(n_lat, gw)
    hmask = jnp.asarray(np.arange(LANES)[None, :] // DK_A == np.arange(2)[:, None], dtype=BF16)

    y_p, y_s = x_prompt, x_sample
    new_a_k, new_a_v, new_c_k, new_c_v = [], [], [], []
    for i in range(depth):
        j = i // 2
        mod_p = mod[i, 0:1]
        mod_s = mod[i, 1:1 + n_dec]
        nw = norm_w[i].reshape(1, d)
        if i % 2 == 0:
            lam_init = 0.8 - 0.6 * math.exp(-0.3 * i)
            w_in = even_w_in[j].astype(BF16)
            w_out = even_w_out[j].astype(BF16)
            poolw = even_pool_w[j].astype(BF16)
            qw = jnp.tile(even_q_norm_w[j], LANES // DK_A).reshape(1, LANES) * (DK_A ** -0.5 * LOG2_E)
            kw = jnp.tile(even_k_norm_w[j], LANES // DK_A).reshape(1, LANES)
            lamv = jnp.stack([even_lam_q1[j], even_lam_k1[j], even_lam_q2[j], even_lam_k2[j]])
            subln = even_subln_w[j].reshape(1, DV_A)
            pscale = even_pool_scale[j].reshape(1, w_b)
            shared = (lamv, hmask, subln, poolw, pscale, w_out, lam_init)

            q, k, v, u, g = _even_in(y_p, mod_p, nw, w_in, qw, kw, None, F32)
            new_a_k.append(k)
            new_a_v.append(v)
            y_p = _even_mix(q, k, v, None, u, g, y_p, mod_p, *pool_p, *shared)
            q, k, v, u, g = _even_in(y_s, mod_s, nw, w_in, qw, kw, rope_a, BF16)
            y_s = _even_mix(q, k, v, (cache_a_k, cache_a_v, j), u, g, y_s, mod_s, *pool_s, *shared)
        else:
            w_in = gqa_w_in[j].astype(BF16)
            w_out = gqa_w_out[j].astype(BF16)
            qw = gqa_q_norm_w[j].reshape(1, HD_C) * (HD_C ** -0.5 * LOG2_E)
            kw = gqa_k_norm_w[j].reshape(1, HD_C)

            q, k, v, g = _odd_in(y_p, mod_p, nw, w_in, qw, kw, None, F32)
            new_c_k.append(k)
            new_c_v.append(v)
            y_p = _odd_mix(q, k, v, None, g, y_p, mod_p, w_out)
            q, k, v, g = _odd_in(y_s, mod_s, nw, w_in, qw, kw, rope_c, BF16)
            y_s = _odd_mix(q, k, v, (cache_c_k, cache_c_v, j), g, y_s, mod_s, w_out)

    return (y_p, y_s, jnp.stack(new_a_k, axis=1), jnp.stack(new_a_v, axis=1),
            jnp.stack(new_c_k, axis=1), jnp.stack(new_c_v, axis=1))
```

```python
import functools
import math

import numpy as np
import jax
import jax.numpy as jnp
from jax import lax
from jax.experimental import pallas as pl
from jax.experimental.pallas import tpu as pltpu

F32 = jnp.float32
BF16 = jnp.bfloat16

GRID_W = 64
ROPE_THETA = 10000.0
EPS = 1e-6
H_A = 8
DK_A = 64
DV_A = 2 * DK_A
W_A = H_A * DV_A
POOL_WINDOWS = (2, 4, 8, 16)
N_POOL = len(POOL_WINDOWS)
HD_C = 128
KVH_C = 4
LOG2_E = math.log2(math.e)

LANES = 128
MXU_DIM = 256
VMEM_LIMIT_BYTES = 56 * 1024 * 1024

TOKEN_TILE = 256
ADA_TILE_N = 1024
SCORE_BYTES_PER_GROUP = 6 * 1024 * 1024


def _heads_per_group(n_heads, score_elems_per_head):
    per = n_heads
    while per > 1 and per * score_elems_per_head * 4 > SCORE_BYTES_PER_GROUP:
        per //= 2
    return per


def _compiler_params(n_axes):
    return pltpu.CompilerParams(
        dimension_semantics=("arbitrary",) * n_axes, vmem_limit_bytes=VMEM_LIMIT_BYTES)


def _resident(block_shape):
    zeros = (0,) * len(block_shape)
    return pl.BlockSpec(block_shape, lambda *_: zeros, pipeline_mode=pl.Buffered(1))


def _silu(x):
    return x / (1.0 + jnp.exp(-x))


def _exp2_rows(s):
    e = jnp.exp2(s - jnp.max(s, axis=-1, keepdims=True))
    return e, 1.0 / jnp.sum(e, axis=-1, keepdims=True)


def _dot(a, b):
    return jnp.dot(a, b, preferred_element_type=F32)


def _dot_nt(a, b):
    return lax.dot_general(a, b, (((1,), (1,)), ((), ())), preferred_element_type=F32)


def _ada_kernel(cond_ref, w_ref, b_ref, o_ref):
    s = _silu(cond_ref[...]).astype(BF16)
    o_ref[0] = _dot(s, w_ref[0].astype(BF16)) + b_ref[0]


def _ada_rows(cond, ada_w, ada_b):
    depth, d, n = ada_w.shape
    rows = cond.shape[0]
    return pl.pallas_call(
        _ada_kernel,
        grid=(depth, n // ADA_TILE_N),
        in_specs=[
            pl.BlockSpec((rows, d), lambda l, j: (0, 0)),
            pl.BlockSpec((1, d, ADA_TILE_N), lambda l, j: (l, 0, j)),
            pl.BlockSpec((1, 1, ADA_TILE_N), lambda l, j: (l, 0, j)),
        ],
        out_specs=pl.BlockSpec((1, rows, ADA_TILE_N), lambda l, j: (l, 0, j)),
        out_shape=jax.ShapeDtypeStruct((depth, rows, n), F32),
        compiler_params=_compiler_params(2),
        name="ada_rows",
    )(cond, ada_w, ada_b.reshape(depth, 1, n))


def _modulated_norm(x, mod_ref, nw_ref):
    r = lax.rsqrt(jnp.mean(x * x, axis=-1, keepdims=True) + EPS)
    a = nw_ref[...] * (1.0 + mod_ref[0, 1:2, :])
    return ((x * r) * a + mod_ref[0, 0:1, :]).astype(BF16)


def _group_rms_scale(t, group):
    sq = t * t
    if group == LANES:
        return lax.rsqrt(jnp.mean(sq, axis=-1, keepdims=True) + EPS)
    assert 2 * group == LANES
    upper = lax.broadcasted_iota(jnp.int32, t.shape, 1) >= group
    lo = jnp.sum(jnp.where(upper, 0.0, sq), axis=-1, keepdims=True)
    hi = jnp.sum(jnp.where(upper, sq, 0.0), axis=-1, keepdims=True)
    return jnp.where(upper, lax.rsqrt(hi * (1.0 / group) + EPS), lax.rsqrt(lo * (1.0 / group) + EPS))


def _rope(t, cos, sin_signed, quarter):
    lane = lax.broadcasted_iota(jnp.int32, t.shape, 1)
    upper = (lane & quarter) != 0
    rot = jnp.where(upper, pltpu.roll(t, quarter, 1), pltpu.roll(t, LANES - quarter, 1))
    return t * cos + rot * sin_signed


def _head_slabs(proj, w_row, group, rope):
    slabs = []
    for j in range(proj.shape[1] // LANES):
        t = proj[:, j * LANES:(j + 1) * LANES]
        slab = t * _group_rms_scale(t, group) * w_row
        if rope is not None:
            slab = _rope(slab, *rope)
        slabs.append(slab)
    return slabs


def _skewed_tiles(n_batch, tiles_per_batch):
    n_tiles = n_batch * tiles_per_batch

    def cur(s):
        c = jnp.minimum(s, n_tiles - 1)
        return c // tiles_per_batch, c % tiles_per_batch

    def prev(s):
        p = jnp.maximum(s - 1, 0)
        return p // tiles_per_batch, p % tiles_per_batch

    return n_tiles, cur, prev


def _start_projection(ymix_ref, ymix_prev_ref, wout_ref, x_ref, mod_ref, y_ref):
    @pl.when(pl.program_id(0) == 0)
    def _():
        ymix_ref[...] = jnp.zeros_like(ymix_ref)

    ymix_prev_ref[...] = ymix_ref[...]

    def project(lo, hi):
        out = _dot(ymix_prev_ref[...], wout_ref[:, lo:hi])
        y_ref[0, :, lo:hi] = x_ref[0, :, lo:hi] + mod_ref[0, 2:3, lo:hi] * out

    return project


def _even_in_kernel(*refs, rope):
    if rope:
        (x_ref, mod_ref, nw_ref, w_ref, qw_ref, kw_ref, cos_ref, sin_ref,
         q_ref, k_ref, v_ref, u_ref, g_ref) = refs
        rope_args = (cos_ref[...], sin_ref[...], DK_A // 4)
    else:
        (x_ref, mod_ref, nw_ref, w_ref, qw_ref, kw_ref,
         q_ref, k_ref, v_ref, u_ref, g_ref) = refs
        rope_args = None
    h = _modulated_norm(x_ref[0], mod_ref, nw_ref)
    w_b = u_ref.shape[2]

    g_ref[0] = _silu(_dot(h, w_ref[:, 3 * W_A + w_b:])).astype(g_ref.dtype)
    q = _dot(h, w_ref[:, 0:W_A])
    for i, slab in enumerate(_head_slabs(q, qw_ref[...], DK_A, rope_args)):
        q_ref[0, i] = slab.astype(q_ref.dtype)
    k = _dot(h, w_ref[:, W_A:2 * W_A])
    for i, slab in enumerate(_head_slabs(k, kw_ref[...], DK_A, rope_args)):
        k_ref[0, i] = slab.astype(k_ref.dtype)
    v = _dot(h, w_ref[:, 2 * W_A:3 * W_A])
    for i in range(H_A):
        v_ref[0, i] = v[:, i * DV_A:(i + 1) * DV_A].astype(v_ref.dtype)
    u_ref[0] = _dot(h, w_ref[:, 3 * W_A:3 * W_A + w_b])


def _even_in(x, mod, norm_w, w_in, qw, kw, rope, kv_dtype):
    b, l, d = x.shape
    n_in = w_in.shape[1]
    w_b = d - W_A
    tl = TOKEN_TILE
    tok = lambda i, t: (i, t, 0)
    head = lambda i, t: (i, 0, t, 0)
    in_specs = [
        pl.BlockSpec((1, tl, d), tok),
        pl.BlockSpec((1, 3, d), lambda i, t: (i if mod.shape[0] > 1 else 0, 0, 0)),
        _resident((1, d)),
        _resident((d, n_in)),
        _resident((1, LANES)),
        _resident((1, LANES)),
    ]
    args = [x, mod, norm_w, w_in, qw, kw]
    if rope is not None:
        in_specs += [pl.BlockSpec((tl, LANES), lambda i, t: (t, 0))] * 2
        args += list(rope)
    head_shape = (b, H_A, l, DV_A)
    return pl.pallas_call(
        functools.partial(_even_in_kernel, rope=rope is not None),
        grid=(b, l // tl),
        in_specs=in_specs,
        out_specs=[
            pl.BlockSpec((1, H_A, tl, DV_A), head),
            pl.BlockSpec((1, H_A, tl, DV_A), head),
            pl.BlockSpec((1, H_A, tl, DV_A), head),
            pl.BlockSpec((1, tl, w_b), tok),
            pl.BlockSpec((1, tl, d), tok),
        ],
        out_shape=[
            jax.ShapeDtypeStruct(head_shape, BF16),
            jax.ShapeDtypeStruct(head_shape, kv_dtype),
            jax.ShapeDtypeStruct(head_shape, kv_dtype),
            jax.ShapeDtypeStruct((b, l, w_b), F32),
            jax.ShapeDtypeStruct((b, l, d), BF16),
        ],
        compiler_params=_compiler_params(2),
        name="even_in",
    )(*args)


def _even_mix_kernel(*refs, has_ctx, lam_init, n_tiles, tiles_per_batch):
    if has_ctx:
        (q_ref, k_ref, v_ref, ck_ref, cv_ref, u_ref, g_ref, x_ref, mod_ref, band_ref, invc_ref,
         lamv_ref, hmask_ref, subln_ref, poolw_ref, pscale_ref, wout_ref, y_ref,
         ymix_ref, ymix_prev_ref) = refs
    else:
        (q_ref, k_ref, v_ref, u_ref, g_ref, x_ref, mod_ref, band_ref, invc_ref,
         lamv_ref, hmask_ref, subln_ref, poolw_ref, pscale_ref, wout_ref, y_ref,
         ymix_ref, ymix_prev_ref) = refs
    tq = q_ref.shape[2]
    l = u_ref.shape[1]
    d = x_ref.shape[2]
    gw = poolw_ref.shape[1]
    project = _start_projection(ymix_ref, ymix_prev_ref, wout_ref, x_ref, mod_ref, y_ref)

    lv = lamv_ref[...]
    lam = (jnp.exp(jnp.sum(lv[0:1] * lv[1:2], axis=-1, keepdims=True))
           - jnp.exp(jnp.sum(lv[2:3] * lv[3:4], axis=-1, keepdims=True)) + lam_init)
    subln = subln_ref[...] * (1.0 - lam_init)

    if l == tq:
        row0 = 0
    else:
        tile = lax.rem(jnp.minimum(pl.program_id(0), n_tiles - 1), tiles_per_batch)
        row0 = pl.multiple_of(tile * tq, tq)
    for j in range(N_POOL):
        lo, hi = j * gw, (j + 1) * gw
        u_all = u_ref[0, :, lo:hi]
        u_own = u_all if l == tq else u_ref[0, pl.ds(row0, tq), lo:hi]
        dev = _dot(band_ref[j], u_all.astype(BF16)) * invc_ref[:, lo:hi] - u_own
        po = _dot(dev.astype(BF16), poolw_ref[j]) * pscale_ref[:, lo:hi]
        ymix_ref[:, W_A + lo:W_A + hi] = (
            po * g_ref[0, :, W_A + lo:W_A + hi].astype(F32)).astype(BF16)

    n_keys = k_ref.shape[2] + (ck_ref.shape[1] if has_ctx else 0)
    per_group = _heads_per_group(H_A, 2 * tq * n_keys)
    n_groups = H_A // per_group
    for gi in range(n_groups):
        heads = range(gi * per_group, (gi + 1) * per_group)
        scores, values = [], []
        for i in heads:
            qh = q_ref[0, i]
            kh = k_ref[0, i].astype(BF16)
            vh = v_ref[0, i].astype(BF16)
            if has_ctx:
                kh = jnp.concatenate([ck_ref[i].astype(BF16), kh], axis=0)
                vh = jnp.concatenate([cv_ref[i].astype(BF16), vh], axis=0)
            scores.append((_dot_nt(qh * hmask_ref[0:1, :], kh), _dot_nt(qh * hmask_ref[1:2, :], kh)))
            values.append(vh)
        project(gi * (d // n_groups), (gi + 1) * (d // n_groups))
        probs = []
        for s1, s2 in scores:
            e1, inv1 = _exp2_rows(s1)
            e2, inv2 = _exp2_rows(s2)
            probs.append((e1 * inv1 - e2 * (lam * inv2)).astype(BF16))
        outs = [_dot(p, vh) for p, vh in zip(probs, values)]
        for i, o in zip(heads, outs):
            lo, hi = i * DV_A, (i + 1) * DV_A
            o = o * lax.rsqrt(jnp.mean(o * o, axis=-1, keepdims=True) + EPS) * subln
            ymix_ref[:, lo:hi] = (o * g_ref[0, :, lo:hi].astype(F32)).astype(BF16)


def _even_mix(q, k, v, ctx, u, g, x, mod, band, invc, lamv, hmask, subln, poolw, pscale, w_out,
              lam_init):
    b, l, d = x.shape
    tq = TOKEN_TILE
    w_b = u.shape[2]
    gw = w_b // N_POOL
    n_tiles, cur, prev = _skewed_tiles(b, l // tq)
    cur_b = lambda s: cur(s)[0]
    in_specs = [
        pl.BlockSpec((1, H_A, tq, DV_A), lambda s: (cur(s)[0], 0, cur(s)[1], 0)),
        pl.BlockSpec((1, H_A, l, DV_A), lambda s: (cur_b(s), 0, 0, 0)),
        pl.BlockSpec((1, H_A, l, DV_A), lambda s: (cur_b(s), 0, 0, 0)),
    ]
    args = [q, k, v]
    if ctx is not None:
        ck, cv, layer = ctx
        p = ck.shape[3]
        in_specs += [pl.BlockSpec((None, None, H_A, p, DV_A), lambda s: (cur_b(s), layer, 0, 0, 0))] * 2
        args += [ck, cv]
    in_specs += [
        pl.BlockSpec((1, l, w_b), lambda s: (cur_b(s), 0, 0)),
        pl.BlockSpec((1, tq, d), lambda s: (*cur(s), 0)),
        pl.BlockSpec((1, tq, d), lambda s: (*prev(s), 0)),
        pl.BlockSpec((1, 3, d), lambda s: (prev(s)[0] if mod.shape[0] > 1 else 0, 0, 0)),
        pl.BlockSpec((N_POOL, tq, l), lambda s: (0, cur(s)[1], 0)),
        pl.BlockSpec((tq, w_b), lambda s: (cur(s)[1], 0)),
        _resident(lamv.shape),
        _resident(hmask.shape),
        _resident((1, DV_A)),
        _resident((N_POOL, gw, gw)),
        _resident((1, w_b)),
        _resident((d, d)),
    ]
    args += [u, g, x, mod, band, invc, lamv, hmask, subln, poolw, pscale, w_out]
    return pl.pallas_call(
        functools.partial(_even_mix_kernel, has_ctx=ctx is not None, lam_init=lam_init,
                          n_tiles=n_tiles, tiles_per_batch=l // tq),
        grid=(n_tiles + 1,),
        in_specs=in_specs,
        out_specs=pl.BlockSpec((1, tq, d), lambda s: (*prev(s), 0)),
        out_shape=jax.ShapeDtypeStruct((b, l, d), F32),
        scratch_shapes=[pltpu.VMEM((tq, d), BF16), pltpu.VMEM((tq, d), BF16)],
        compiler_params=_compiler_params(1),
        name="even_mix",
    )(*args)


def _odd_in_kernel(*refs, rope):
    if rope:
        (x_ref, mod_ref, nw_ref, w_ref, qw_ref, kw_ref, cos_ref, sin_ref,
         q_ref, k_ref, v_ref, g_ref) = refs
        rope_args = (cos_ref[...], sin_ref[...], HD_C // 4)
    else:
        (x_ref, mod_ref, nw_ref, w_ref, qw_ref, kw_ref,
         q_ref, k_ref, v_ref, g_ref) = refs
        rope_args = None
    h = _modulated_norm(x_ref[0], mod_ref, nw_ref)
    w_c = g_ref.shape[2]
    w_kv = KVH_C * HD_C

    g_ref[0] = _silu(_dot(h, w_ref[:, w_c + 2 * w_kv:])).astype(g_ref.dtype)
    q = _dot(h, w_ref[:, 0:w_c])
    for i, slab in enumerate(_head_slabs(q, qw_ref[...], HD_C, rope_args)):
        q_ref[0, i] = slab.astype(q_ref.dtype)
    k = _dot(h, w_ref[:, w_c:w_c + w_kv])
    for i, slab in enumerate(_head_slabs(k, kw_ref[...], HD_C, rope_args)):
        k_ref[0, i] = slab.astype(k_ref.dtype)
    v = _dot(h, w_ref[:, w_c + w_kv:w_c + 2 * w_kv])
    for i in range(KVH_C):
        v_ref[0, i] = v[:, i * HD_C:(i + 1) * HD_C].astype(v_ref.dtype)


def _odd_in(x, mod, norm_w, w_in, qw, kw, rope, kv_dtype):
    b, l, d = x.shape
    n_in = w_in.shape[1]
    h_c = d // HD_C
    tl = TOKEN_TILE
    tok = lambda i, t: (i, t, 0)
    head = lambda i, t: (i, 0, t, 0)
    in_specs = [
        pl.BlockSpec((1, tl, d), tok),
        pl.BlockSpec((1, 3, d), lambda i, t: (i if mod.shape[0] > 1 else 0, 0, 0)),
        _resident((1, d)),
        _resident((d, n_in)),
        _resident((1, LANES)),
        _resident((1, LANES)),
    ]
    args = [x, mod, norm_w, w_in, qw, kw]
    if rope is not None:
        in_specs += [pl.BlockSpec((tl, LANES), lambda i, t: (t, 0))] * 2
        args += list(rope)
    return pl.pallas_call(
        functools.partial(_odd_in_kernel, rope=rope is not None),
        grid=(b, l // tl),
        in_specs=in_specs,
        out_specs=[
            pl.BlockSpec((1, h_c, tl, HD_C), head),
            pl.BlockSpec((1, KVH_C, tl, HD_C), head),
            pl.BlockSpec((1, KVH_C, tl, HD_C), head),
            pl.BlockSpec((1, tl, d), tok),
        ],
        out_shape=[
            jax.ShapeDtypeStruct((b, h_c, l, HD_C), BF16),
            jax.ShapeDtypeStruct((b, KVH_C, l, HD_C), kv_dtype),
            jax.ShapeDtypeStruct((b, KVH_C, l, HD_C), kv_dtype),
            jax.ShapeDtypeStruct((b, l, d), BF16),
        ],
        compiler_params=_compiler_params(2),
        name="odd_in",
    )(*args)


def _odd_mix_kernel(*refs, has_ctx):
    if has_ctx:
        (q_ref, k_ref, v_ref, ck_ref, cv_ref, g_ref, x_ref, mod_ref, wout_ref, y_ref,
         ymix_ref, ymix_prev_ref) = refs
    else:
        q_ref, k_ref, v_ref, g_ref, x_ref, mod_ref, wout_ref, y_ref, ymix_ref, ymix_prev_ref = refs
    h_c, tq = q_ref.shape[1], q_ref.shape[2]
    d = x_ref.shape[2]
    rep = h_c // KVH_C
    project = _start_projection(ymix_ref, ymix_prev_ref, wout_ref, x_ref, mod_ref, y_ref)

    n_keys = k_ref.shape[2] + (ck_ref.shape[1] if has_ctx else 0)
    per_group = _heads_per_group(KVH_C, rep * tq * n_keys)
    n_groups = KVH_C // per_group
    for gi in range(n_groups):
        kv_heads = range(gi * per_group, (gi + 1) * per_group)
        scores, values = [], []
        for j in kv_heads:
            kh = k_ref[0, j].astype(BF16)
            vh = v_ref[0, j].astype(BF16)
            if has_ctx:
                kh = jnp.concatenate([ck_ref[j].astype(BF16), kh], axis=0)
                vh = jnp.concatenate([cv_ref[j].astype(BF16), vh], axis=0)
            qs = q_ref[0, j * rep:(j + 1) * rep].reshape(rep * tq, HD_C)
            scores.append(_dot_nt(qs, kh))
            values.append(vh)
        project(gi * (d // n_groups), (gi + 1) * (d // n_groups))
        exps = [_exp2_rows(s) for s in scores]
        outs = [_dot(e.astype(BF16), vh) * inv for (e, inv), vh in zip(exps, values)]
        for j, o in zip(kv_heads, outs):
            for r in range(rep):
                lo, hi = (j * rep + r) * HD_C, (j * rep + r + 1) * HD_C
                ymix_ref[:, lo:hi] = (
                    o[r * tq:(r + 1) * tq] * g_ref[0, :, lo:hi].astype(F32)).astype(BF16)


def _odd_mix(q, k, v, ctx, g, x, mod, w_out):
    b, l, d = x.shape
    h_c = q.shape[1]
    tq = TOKEN_TILE
    n_tiles, cur, prev = _skewed_tiles(b, l // tq)
    cur_b = lambda s: cur(s)[0]
    in_specs = [
        pl.BlockSpec((1, h_c, tq, HD_C), lambda s: (cur(s)[0], 0, cur(s)[1], 0)),
        pl.BlockSpec((1, KVH_C, l, HD_C), lambda s: (cur_b(s), 0, 0, 0)),
        pl.BlockSpec((1, KVH_C, l, HD_C), lambda s: (cur_b(s), 0, 0, 0)),
    ]
    args = [q, k, v]
    if ctx is not None:
        ck, cv, layer = ctx
        p = ck.shape[3]
        in_specs += [pl.BlockSpec((None, None, KVH_C, p, HD_C), lambda s: (cur_b(s), layer, 0, 0, 0))] * 2
        args += [ck, cv]
    in_specs += [
        pl.BlockSpec((1, tq, d), lambda s: (*cur(s), 0)),
        pl.BlockSpec((1, tq, d), lambda s: (*prev(s), 0)),
        pl.BlockSpec((1, 3, d), lambda s: (prev(s)[0] if mod.shape[0] > 1 else 0, 0, 0)),
        _resident((d, d)),
    ]
    args += [g, x, mod, w_out]
    return pl.pallas_call(
        functools.partial(_odd_mix_kernel, has_ctx=ctx is not None),
        grid=(n_tiles + 1,),
        in_specs=in_specs,
        out_specs=pl.BlockSpec((1, tq, d), lambda s: (*prev(s), 0)),
        out_shape=jax.ShapeDtypeStruct((b, l, d), F32),
        scratch_shapes=[pltpu.VMEM((tq, d), BF16), pltpu.VMEM((tq, d), BF16)],
        compiler_params=_compiler_params(1),
        name="odd_mix",
    )(*args)


def _rope_tables(n_tokens, dim):
    rows = n_tokens // GRID_W
    row = jnp.repeat(jnp.arange(rows), GRID_W).astype(F32)
    col = jnp.tile(jnp.arange(GRID_W), rows).astype(F32)
    quarter = dim // 4
    freqs = ROPE_THETA ** (-jnp.arange(quarter, dtype=F32) / quarter)
    ar = row[:, None] * freqs
    ac = col[:, None] * freqs
    cos = jnp.concatenate([jnp.cos(ar), jnp.cos(ar), jnp.cos(ac), jnp.cos(ac)], axis=-1)
    sin = jnp.concatenate([-jnp.sin(ar), jnp.sin(ar), -jnp.sin(ac), jnp.sin(ac)], axis=-1)
    reps = LANES // dim
    return jnp.tile(cos, (1, reps)), jnp.tile(sin, (1, reps))


def _pool_tables(l, gw):
    t = np.arange(l)[:, None]
    s = np.arange(l)[None, :]
    bands, invs = [], []
    for w in POOL_WINDOWS:
        lo = np.clip(t - w // 2, 0, l)
        hi = np.clip(t + w // 2, 0, l)
        bands.append((s >= lo) & (s < hi))
        invs.append(np.repeat(1.0 / (hi - lo).astype(np.float32), gw, axis=1))
    return (jnp.asarray(np.stack(bands), dtype=BF16),
            jnp.asarray(np.concatenate(invs, axis=1), dtype=F32))


def kernel(x_prompt, x_sample, cache_a_k, cache_a_v, cache_c_k, cache_c_v, c, c_ctx, norm_w, ada_w, ada_b, even_w_in, even_q_norm_w, even_k_norm_w, even_lam_q1, even_lam_k1, even_lam_q2, even_lam_k2, even_subln_w, even_pool_w, even_pool_scale, even_w_out, gqa_w_in, gqa_q_norm_w, gqa_k_norm_w, gqa_w_out):
    depth, d = norm_w.shape
    n_lat = x_sample.shape[1]
    n_dec = x_sample.shape[0]
    w_b = d - W_A
    gw = w_b // N_POOL

    rows = -(-(1 + n_dec) // 8) * 8
    cond = jnp.zeros((rows, d), F32).at[0].set(c_ctx).at[1:1 + n_dec].set(c)
    mod = _ada_rows(cond, ada_w, ada_b).reshape(depth, rows, 3, d)

    rope_a = _rope_tables(n_lat, DK_A)
    rope_c = _rope_tables(n_lat, HD_C)
    pool_p = _pool_tables(x_prompt.shape[1], gw)
    pool_s = _pool_tables(n_lat, gw)
    hmask = jnp.asarray(np.arange(LANES)[None, :] // DK_A == np.arange(2)[:, None], dtype=BF16)

    y_p, y_s = x_prompt, x_sample
    new_a_k, new_a_v, new_c_k, new_c_v = [], [], [], []
    for i in range(depth):
        j = i // 2
        mod_p = mod[i, 0:1]
        mod_s = mod[i, 1:1 + n_dec]
        nw = norm_w[i].reshape(1, d)
        if i % 2 == 0:
            lam_init = 0.8 - 0.6 * math.exp(-0.3 * i)
            w_in = even_w_in[j].astype(BF16)
            w_out = even_w_out[j].astype(BF16)
            poolw = even_pool_w[j].astype(BF16)
            qw = jnp.tile(even_q_norm_w[j], LANES // DK_A).reshape(1, LANES) * (DK_A ** -0.5 * LOG2_E)
            kw = jnp.tile(even_k_norm_w[j], LANES // DK_A).reshape(1, LANES)
            lamv = jnp.stack([even_lam_q1[j], even_lam_k1[j], even_lam_q2[j], even_lam_k2[j]])
            subln = even_subln_w[j].reshape(1, DV_A)
            pscale = even_pool_scale[j].reshape(1, w_b)
            shared = (lamv, hmask, subln, poolw, pscale, w_out, lam_init)

            q, k, v, u, g = _even_in(y_p, mod_p, nw, w_in, qw, kw, None, F32)
            new_a_k.append(k)
            new_a_v.append(v)
            y_p = _even_mix(q, k, v, None, u, g, y_p, mod_p, *pool_p, *shared)
            q, k, v, u, g = _even_in(y_s, mod_s, nw, w_in, qw, kw, rope_a, BF16)
            y_s = _even_mix(q, k, v, (cache_a_k, cache_a_v, j), u, g, y_s, mod_s, *pool_s, *shared)
        else:
            w_in = gqa_w_in[j].astype(BF16)
            w_out = gqa_w_out[j].astype(BF16)
            qw = gqa_q_norm_w[j].reshape(1, HD_C) * (HD_C ** -0.5 * LOG2_E)
            kw = gqa_k_norm_w[j].reshape(1, HD_C)

            q, k, v, g = _odd_in(y_p, mod_p, nw, w_in, qw, kw, None, F32)
            new_c_k.append(k)
            new_c_v.append(v)
            y_p = _odd_mix(q, k, v, None, g, y_p, mod_p, w_out)
            q, k, v, g = _odd_in(y_s, mod_s, nw, w_in, qw, kw, rope_c, BF16)
            y_s = _odd_mix(q, k, v, (cache_c_k, cache_c_v, j), g, y_s, mod_s, w_out)

    return (y_p, y_s, jnp.stack(new_a_k, axis=1), jnp.stack(new_a_v, axis=1),
            jnp.stack(new_c_k, axis=1), jnp.stack(new_c_v, axis=1))
```

```python
import functools
import math

import numpy as np
import jax
import jax.numpy as jnp
from jax import lax
from jax.experimental import pallas as pl
from jax.experimental.pallas import tpu as pltpu

F32 = jnp.float32
BF16 = jnp.bfloat16

GRID_W = 64
ROPE_THETA = 10000.0
EPS = 1e-6
H_A = 8
DK_A = 64
DV_A = 2 * DK_A
W_A = H_A * DV_A
POOL_WINDOWS = (2, 4, 8, 16)
N_POOL = len(POOL_WINDOWS)
HD_C = 128
KVH_C = 4
LOG2_E = math.log2(math.e)

LANES = 128
MXU_DIM = 256
VMEM_LIMIT_BYTES = 56 * 1024 * 1024

TOKEN_TILE = 256
ADA_TILE_N = 1024
SCORE_BYTES_PER_GROUP = 6 * 1024 * 1024
MIN_HEAD_GROUPS = 4


def _heads_per_group(n_heads, score_elems_per_head):
    per = max(1, n_heads // MIN_HEAD_GROUPS)
    while per > 1 and per * score_elems_per_head * 4 > SCORE_BYTES_PER_GROUP:
        per //= 2
    return per


def _compiler_params(n_axes):
    return pltpu.CompilerParams(
        dimension_semantics=("arbitrary",) * n_axes, vmem_limit_bytes=VMEM_LIMIT_BYTES)


def _resident(block_shape):
    zeros = (0,) * len(block_shape)
    return pl.BlockSpec(block_shape, lambda *_: zeros, pipeline_mode=pl.Buffered(1))


def _mod_spec(mod_sel, batch_of):
    mod_all, layer, row0, per_batch = mod_sel
    return pl.BlockSpec(
        (None, 1, 3, mod_all.shape[-1]),
        lambda *idx: (layer, row0 + (batch_of(*idx) if per_batch else 0), 0, 0))


def _silu(x):
    return x / (1.0 + jnp.exp(-x))


def _exp2_rows(s):
    e = jnp.exp2(s - jnp.max(s, axis=-1, keepdims=True))
    return e, 1.0 / jnp.sum(e, axis=-1, keepdims=True)


def _dot(a, b):
    return jnp.dot(a, b, preferred_element_type=F32)


def _dot_nt(a, b):
    return lax.dot_general(a, b, (((1,), (1,)), ((), ())), preferred_element_type=F32)


def _ada_kernel(cond_ref, w_ref, b_ref, o_ref):
    s = _silu(cond_ref[...]).astype(BF16)
    o_ref[0] = _dot(s, w_ref[0].astype(BF16)) + b_ref[0]


def _ada_rows(cond, ada_w, ada_b):
    depth, d, n = ada_w.shape
    rows = cond.shape[0]
    return pl.pallas_call(
        _ada_kernel,
        grid=(depth, n // ADA_TILE_N),
        in_specs=[
            pl.BlockSpec((rows, d), lambda l, j: (0, 0)),
            pl.BlockSpec((1, d, ADA_TILE_N), lambda l, j: (l, 0, j)),
            pl.BlockSpec((1, 1, ADA_TILE_N), lambda l, j: (l, 0, j)),
        ],
        out_specs=pl.BlockSpec((1, rows, ADA_TILE_N), lambda l, j: (l, 0, j)),
        out_shape=jax.ShapeDtypeStruct((depth, rows, n), F32),
        compiler_params=_compiler_params(2),
        name="ada_rows",
    )(cond, ada_w, ada_b.reshape(depth, 1, n))


def _modulated_norm(x, mod_ref, nw_ref):
    r = lax.rsqrt(jnp.mean(x * x, axis=-1, keepdims=True) + EPS)
    a = nw_ref[...] * (1.0 + mod_ref[0, 1:2, :])
    return ((x * r) * a + mod_ref[0, 0:1, :]).astype(BF16)


def _group_rms_scale(t, group):
    sq = t * t
    if group == LANES:
        return lax.rsqrt(jnp.mean(sq, axis=-1, keepdims=True) + EPS)
    assert 2 * group == LANES
    upper = lax.broadcasted_iota(jnp.int32, t.shape, 1) >= group
    lo = jnp.sum(jnp.where(upper, 0.0, sq), axis=-1, keepdims=True)
    hi = jnp.sum(jnp.where(upper, sq, 0.0), axis=-1, keepdims=True)
    return jnp.where(upper, lax.rsqrt(hi * (1.0 / group) + EPS), lax.rsqrt(lo * (1.0 / group) + EPS))


def _rope(t, cos, sin_signed, quarter):
    lane = lax.broadcasted_iota(jnp.int32, t.shape, 1)
    upper = (lane & quarter) != 0
    rot = jnp.where(upper, pltpu.roll(t, quarter, 1), pltpu.roll(t, LANES - quarter, 1))
    return t * cos + rot * sin_signed


def _head_slabs(proj, w_row, group, rope):
    slabs = []
    for j in range(proj.shape[1] // LANES):
        t = proj[:, j * LANES:(j + 1) * LANES]
        slab = t * _group_rms_scale(t, group) * w_row
        if rope is not None:
            slab = _rope(slab, *rope)
        slabs.append(slab)
    return slabs


def _skewed_tiles(n_batch, tiles_per_batch):
    n_tiles = n_batch * tiles_per_batch

    def cur(s):
        c = jnp.minimum(s, n_tiles - 1)
        return c // tiles_per_batch, c % tiles_per_batch

    def prev(s):
        p = jnp.maximum(s - 1, 0)
        return p // tiles_per_batch, p % tiles_per_batch

    return n_tiles, cur, prev


def _start_projection(ymix_ref, ymix_prev_ref, wout_ref, x_ref, mod_ref, y_ref):
    @pl.when(pl.program_id(0) == 0)
    def _():
        ymix_ref[...] = jnp.zeros_like(ymix_ref)

    ymix_prev_ref[...] = ymix_ref[...]

    def project(lo, hi):
        out = _dot(ymix_prev_ref[...], wout_ref[:, lo:hi])
        y_ref[0, :, lo:hi] = x_ref[0, :, lo:hi] + mod_ref[0, 2:3, lo:hi] * out

    return project


def _even_in_kernel(*refs, rope):
    if rope:
        (x_ref, mod_ref, nw_ref, w_ref, qw_ref, kw_ref, cos_ref, sin_ref,
         q_ref, k_ref, v_ref, u_ref, g_ref) = refs
        rope_args = (cos_ref[...], sin_ref[...], DK_A // 4)
    else:
        (x_ref, mod_ref, nw_ref, w_ref, qw_ref, kw_ref,
         q_ref, k_ref, v_ref, u_ref, g_ref) = refs
        rope_args = None
    h = _modulated_norm(x_ref[0], mod_ref, nw_ref)
    w_b = u_ref.shape[2]

    g_ref[0] = _silu(_dot(h, w_ref[:, 3 * W_A + w_b:])).astype(g_ref.dtype)
    q = _dot(h, w_ref[:, 0:W_A])
    for i, slab in enumerate(_head_slabs(q, qw_ref[...], DK_A, rope_args)):
        q_ref[0, i] = slab.astype(q_ref.dtype)
    k = _dot(h, w_ref[:, W_A:2 * W_A])
    for i, slab in enumerate(_head_slabs(k, kw_ref[...], DK_A, rope_args)):
        k_ref[0, i] = slab.astype(k_ref.dtype)
    v = _dot(h, w_ref[:, 2 * W_A:3 * W_A])
    for i in range(H_A):
        v_ref[0, i] = v[:, i * DV_A:(i + 1) * DV_A].astype(v_ref.dtype)
    u_ref[0] = _dot(h, w_ref[:, 3 * W_A:3 * W_A + w_b])


def _even_in(x, mod, norm_w, w_in, qw, kw, rope, kv_dtype):
    b, l, d = x.shape
    n_in = w_in.shape[1]
    w_b = d - W_A
    tl = TOKEN_TILE
    tok = lambda i, t: (i, t, 0)
    head = lambda i, t: (i, 0, t, 0)
    in_specs = [
        pl.BlockSpec((1, tl, d), tok),
        _mod_spec(mod, lambda i, t: i),
        _resident((1, d)),
        _resident((d, n_in)),
        _resident((1, LANES)),
        _resident((1, LANES)),
    ]
    args = [x, mod[0], norm_w, w_in, qw, kw]
    if rope is not None:
        in_specs += [pl.BlockSpec((tl, LANES), lambda i, t: (t, 0))] * 2
        args += list(rope)
    head_shape = (b, H_A, l, DV_A)
    return pl.pallas_call(
        functools.partial(_even_in_kernel, rope=rope is not None),
        grid=(b, l // tl),
        in_specs=in_specs,
        out_specs=[
            pl.BlockSpec((1, H_A, tl, DV_A), head),
            pl.BlockSpec((1, H_A, tl, DV_A), head),
            pl.BlockSpec((1, H_A, tl, DV_A), head),
            pl.BlockSpec((1, tl, w_b), tok),
            pl.BlockSpec((1, tl, d), tok),
        ],
        out_shape=[
            jax.ShapeDtypeStruct(head_shape, BF16),
            jax.ShapeDtypeStruct(head_shape, kv_dtype),
            jax.ShapeDtypeStruct(head_shape, kv_dtype),
            jax.ShapeDtypeStruct((b, l, w_b), F32),
            jax.ShapeDtypeStruct((b, l, d), BF16),
        ],
        compiler_params=_compiler_params(2),
        name="even_in",
    )(*args)


def _even_mix_kernel(*refs, has_ctx, lam_init, n_tiles, tiles_per_batch):
    if has_ctx:
        (q_ref, k_ref, v_ref, ck_ref, cv_ref, u_ref, g_ref, x_ref, mod_ref, band_ref, invc_ref,
         lamv_ref, hmask_ref, subln_ref, poolw_ref, pscale_ref, wout_ref, y_ref,
         ymix_ref, ymix_prev_ref) = refs
    else:
        (q_ref, k_ref, v_ref, u_ref, g_ref, x_ref, mod_ref, band_ref, invc_ref,
         lamv_ref, hmask_ref, subln_ref, poolw_ref, pscale_ref, wout_ref, y_ref,
         ymix_ref, ymix_prev_ref) = refs
    tq = q_ref.shape[2]
    l = u_ref.shape[1]
    d = x_ref.shape[2]
    gw = poolw_ref.shape[1]
    project = _start_projection(ymix_ref, ymix_prev_ref, wout_ref, x_ref, mod_ref, y_ref)

    lv = lamv_ref[...]
    lam = (jnp.exp(jnp.sum(lv[0:1] * lv[1:2], axis=-1, keepdims=True))
           - jnp.exp(jnp.sum(lv[2:3] * lv[3:4], axis=-1, keepdims=True)) + lam_init)
    subln = subln_ref[...] * (1.0 - lam_init)

    if l == tq:
        row0 = 0
    else:
        tile = lax.rem(jnp.minimum(pl.program_id(0), n_tiles - 1), tiles_per_batch)
        row0 = pl.multiple_of(tile * tq, tq)
    for j in range(N_POOL):
        lo, hi = j * gw, (j + 1) * gw
        u_all = u_ref[0, :, lo:hi]
        u_own = u_all if l == tq else u_ref[0, pl.ds(row0, tq), lo:hi]
        dev = _dot(band_ref[j], u_all.astype(BF16)) * invc_ref[:, lo:hi] - u_own
        po = _dot(dev.astype(BF16), poolw_ref[j]) * pscale_ref[:, lo:hi]
        ymix_ref[:, W_A + lo:W_A + hi] = (
            po * g_ref[0, :, W_A + lo:W_A + hi].astype(F32)).astype(BF16)

    n_keys = k_ref.shape[2] + (ck_ref.shape[1] if has_ctx else 0)
    per_group = _heads_per_group(H_A, 2 * tq * n_keys)
    n_groups = H_A // per_group
    for gi in range(n_groups):
        heads = range(gi * per_group, (gi + 1) * per_group)
        scores, values = [], []
        for i in heads:
            qh = q_ref[0, i]
            kh = k_ref[0, i].astype(BF16)
            vh = v_ref[0, i].astype(BF16)
            if has_ctx:
                kh = jnp.concatenate([ck_ref[i].astype(BF16), kh], axis=0)
                vh = jnp.concatenate([cv_ref[i].astype(BF16), vh], axis=0)
            scores.append((_dot_nt(qh * hmask_ref[0:1, :], kh), _dot_nt(qh * hmask_ref[1:2, :], kh)))
            values.append(vh)
        project(gi * (d // n_groups), (gi + 1) * (d // n_groups))
        probs = []
        for s1, s2 in scores:
            e1, inv1 = _exp2_rows(s1)
            e2, inv2 = _exp2_rows(s2)
            probs.append((e1 * inv1 - e2 * (lam * inv2)).astype(BF16))
        outs = [_dot(p, vh) for p, vh in zip(probs, values)]
        for i, o in zip(heads, outs):
            lo, hi = i * DV_A, (i + 1) * DV_A
            o = o * lax.rsqrt(jnp.mean(o * o, axis=-1, keepdims=True) + EPS) * subln
            ymix_ref[:, lo:hi] = (o * g_ref[0, :, lo:hi].astype(F32)).astype(BF16)


def _even_mix(q, k, v, ctx, u, g, x, mod, band, invc, lamv, hmask, subln, poolw, pscale, w_out,
              lam_init):
    b, l, d = x.shape
    tq = TOKEN_TILE
    w_b = u.shape[2]
    gw = w_b // N_POOL
    n_tiles, cur, prev = _skewed_tiles(b, l // tq)
    cur_b = lambda s: cur(s)[0]
    in_specs = [
        pl.BlockSpec((1, H_A, tq, DV_A), lambda s: (cur(s)[0], 0, cur(s)[1], 0)),
        pl.BlockSpec((1, H_A, l, DV_A), lambda s: (cur_b(s), 0, 0, 0)),
        pl.BlockSpec((1, H_A, l, DV_A), lambda s: (cur_b(s), 0, 0, 0)),
    ]
    args = [q, k, v]
    if ctx is not None:
        ck, cv, layer = ctx
        p = ck.shape[3]
        in_specs += [pl.BlockSpec((None, None, H_A, p, DV_A), lambda s: (cur_b(s), layer, 0, 0, 0))] * 2
        args += [ck, cv]
    in_specs += [
        pl.BlockSpec((1, l, w_b), lambda s: (cur_b(s), 0, 0)),
        pl.BlockSpec((1, tq, d), lambda s: (*cur(s), 0)),
        pl.BlockSpec((1, tq, d), lambda s: (*prev(s), 0)),
        _mod_spec(mod, lambda s: prev(s)[0]),
        pl.BlockSpec((N_POOL, tq, l), lambda s: (0, cur(s)[1], 0)),
        pl.BlockSpec((tq, w_b), lambda s: (cur(s)[1], 0)),
        _resident(lamv.shape),
        _resident(hmask.shape),
        _resident((1, DV_A)),
        _resident((N_POOL, gw, gw)),
        _resident((1, w_b)),
        _resident((d, d)),
    ]
    args += [u, g, x, mod[0], band, invc, lamv, hmask, subln, poolw, pscale, w_out]
    return pl.pallas_call(
        functools.partial(_even_mix_kernel, has_ctx=ctx is not None, lam_init=lam_init,
                          n_tiles=n_tiles, tiles_per_batch=l // tq),
        grid=(n_tiles + 1,),
        in_specs=in_specs,
        out_specs=pl.BlockSpec((1, tq, d), lambda s: (*prev(s), 0)),
        out_shape=jax.ShapeDtypeStruct((b, l, d), F32),
        scratch_shapes=[pltpu.VMEM((tq, d), BF16), pltpu.VMEM((tq, d), BF16)],
        compiler_params=_compiler_params(1),
        name="even_mix",
    )(*args)


def _odd_in_kernel(*refs, rope):
    if rope:
        (x_ref, mod_ref, nw_ref, w_ref, qw_ref, kw_ref, cos_ref, sin_ref,
         q_ref, k_ref, v_ref, g_ref) = refs
        rope_args = (cos_ref[...], sin_ref[...], HD_C // 4)
    else:
        (x_ref, mod_ref, nw_ref, w_ref, qw_ref, kw_ref,
         q_ref, k_ref, v_ref, g_ref) = refs
        rope_args = None
    h = _modulated_norm(x_ref[0], mod_ref, nw_ref)
    w_c = g_ref.shape[2]
    w_kv = KVH_C * HD_C

    g_ref[0] = _silu(_dot(h, w_ref[:, w_c + 2 * w_kv:])).astype(g_ref.dtype)
    q = _dot(h, w_ref[:, 0:w_c])
    for i, slab in enumerate(_head_slabs(q, qw_ref[...], HD_C, rope_args)):
        q_ref[0, i] = slab.astype(q_ref.dtype)
    k = _dot(h, w_ref[:, w_c:w_c + w_kv])
    for i, slab in enumerate(_head_slabs(k, kw_ref[...], HD_C, rope_args)):
        k_ref[0, i] = slab.astype(k_ref.dtype)
    v = _dot(h, w_ref[:, w_c + w_kv:w_c + 2 * w_kv])
    for i in range(KVH_C):
        v_ref[0, i] = v[:, i * HD_C:(i + 1) * HD_C].astype(v_ref.dtype)


def _odd_in(x, mod, norm_w, w_in, qw, kw, rope, kv_dtype):
    b, l, d = x.shape
    n_in = w_in.shape[1]
    h_c = d // HD_C
    tl = TOKEN_TILE
    tok = lambda i, t: (i, t, 0)
    head = lambda i, t: (i, 0, t, 0)
    in_specs = [
        pl.BlockSpec((1, tl, d), tok),
        _mod_spec(mod, lambda i, t: i),
        _resident((1, d)),
        _resident((d, n_in)),
        _resident((1, LANES)),
        _resident((1, LANES)),
    ]
    args = [x, mod[0], norm_w, w_in, qw, kw]
    if rope is not None:
        in_specs += [pl.BlockSpec((tl, LANES), lambda i, t: (t, 0))] * 2
        args += list(rope)
    return pl.pallas_call(
        functools.partial(_odd_in_kernel, rope=rope is not None),
        grid=(b, l // tl),
        in_specs=in_specs,
        out_specs=[
            pl.BlockSpec((1, h_c, tl, HD_C), head),
            pl.BlockSpec((1, KVH_C, tl, HD_C), head),
            pl.BlockSpec((1, KVH_C, tl, HD_C), head),
            pl.BlockSpec((1, tl, d), tok),
        ],
        out_shape=[
            jax.ShapeDtypeStruct((b, h_c, l, HD_C), BF16),
            jax.ShapeDtypeStruct((b, KVH_C, l, HD_C), kv_dtype),
            jax.ShapeDtypeStruct((b, KVH_C, l, HD_C), kv_dtype),
            jax.ShapeDtypeStruct((b, l, d), BF16),
        ],
        compiler_params=_compiler_params(2),
        name="odd_in",
    )(*args)


def _odd_mix_kernel(*refs, has_ctx):
    if has_ctx:
        (q_ref, k_ref, v_ref, ck_ref, cv_ref, g_ref, x_ref, mod_ref, wout_ref, y_ref,
         ymix_ref, ymix_prev_ref) = refs
    else:
        q_ref, k_ref, v_ref, g_ref, x_ref, mod_ref, wout_ref, y_ref, ymix_ref, ymix_prev_ref = refs
    h_c, tq = q_ref.shape[1], q_ref.shape[2]
    d = x_ref.shape[2]
    rep = h_c // KVH_C
    project = _start_projection(ymix_ref, ymix_prev_ref, wout_ref, x_ref, mod_ref, y_ref)

    n_keys = k_ref.shape[2] + (ck_ref.shape[1] if has_ctx else 0)
    per_group = _heads_per_group(KVH_C, rep * tq * n_keys)
    n_groups = KVH_C // per_group
    for gi in range(n_groups):
        kv_heads = range(gi * per_group, (gi + 1) * per_group)
        scores, values = [], []
        for j in kv_heads:
            kh = k_ref[0, j].astype(BF16)
            vh = v_ref[0, j].astype(BF16)
            if has_ctx:
                kh = jnp.concatenate([ck_ref[j].astype(BF16), kh], axis=0)
                vh = jnp.concatenate([cv_ref[j].astype(BF16), vh], axis=0)
            qs = q_ref[0, j * rep:(j + 1) * rep].reshape(rep * tq, HD_C)
            scores.append(_dot_nt(qs, kh))
            values.append(vh)
        project(gi * (d // n_groups), (gi + 1) * (d // n_groups))
        exps = [_exp2_rows(s) for s in scores]
        outs = [_dot(e.astype(BF16), vh) * inv for (e, inv), vh in zip(exps, values)]
        for j, o in zip(kv_heads, outs):
            for r in range(rep):
                lo, hi = (j * rep + r) * HD_C, (j * rep + r + 1) * HD_C
                ymix_ref[:, lo:hi] = (
                    o[r * tq:(r + 1) * tq] * g_ref[0, :, lo:hi].astype(F32)).astype(BF16)


def _odd_mix(q, k, v, ctx, g, x, mod, w_out):
    b, l, d = x.shape
    h_c = q.shape[1]
    tq = TOKEN_TILE
    n_tiles, cur, prev = _skewed_tiles(b, l // tq)
    cur_b = lambda s: cur(s)[0]
    in_specs = [
        pl.BlockSpec((1, h_c, tq, HD_C), lambda s: (cur(s)[0], 0, cur(s)[1], 0)),
        pl.BlockSpec((1, KVH_C, l, HD_C), lambda s: (cur_b(s), 0, 0, 0)),
        pl.BlockSpec((1, KVH_C, l, HD_C), lambda s: (cur_b(s), 0, 0, 0)),
    ]
    args = [q, k, v]
    if ctx is not None:
        ck, cv, layer = ctx
        p = ck.shape[3]
        in_specs += [pl.BlockSpec((None, None, KVH_C, p, HD_C), lambda s: (cur_b(s), layer, 0, 0, 0))] * 2
        args += [ck, cv]
    in_specs += [
        pl.BlockSpec((1, tq, d), lambda s: (*cur(s), 0)),
        pl.BlockSpec((1, tq, d), lambda s: (*prev(s), 0)),
        _mod_spec(mod, lambda s: prev(s)[0]),
        _resident((d, d)),
    ]
    args += [g, x, mod[0], w_out]
    return pl.pallas_call(
        functools.partial(_odd_mix_kernel, has_ctx=ctx is not None),
        grid=(n_tiles + 1,),
        in_specs=in_specs,
        out_specs=pl.BlockSpec((1, tq, d), lambda s: (*prev(s), 0)),
        out_shape=jax.ShapeDtypeStruct((b, l, d), F32),
        scratch_shapes=[pltpu.VMEM((tq, d), BF16), pltpu.VMEM((tq, d), BF16)],
        compiler_params=_compiler_params(1),
        name="odd_mix",
    )(*args)


def _rope_tables(n_tokens, dim):
    rows = n_tokens // GRID_W
    row = np.repeat(np.arange(rows), GRID_W).astype(np.float32)
    col = np.tile(np.arange(GRID_W), rows).astype(np.float32)
    quarter = dim // 4
    freqs = (ROPE_THETA ** (-np.arange(quarter, dtype=np.float32) / quarter)).astype(np.float32)
    ar = row[:, None] * freqs
    ac = col[:, None] * freqs
    cos = np.concatenate([np.cos(ar), np.cos(ar), np.cos(ac), np.cos(ac)], axis=-1)
    sin = np.concatenate([-np.sin(ar), np.sin(ar), -np.sin(ac), np.sin(ac)], axis=-1)
    reps = LANES // dim
    return (jnp.asarray(np.tile(cos, (1, reps)), dtype=F32),
            jnp.asarray(np.tile(sin, (1, reps)), dtype=F32))


def _pool_tables(l, gw):
    t = np.arange(l)[:, None]
    s = np.arange(l)[None, :]
    bands, invs = [], []
    for w in POOL_WINDOWS:
        lo = np.clip(t - w // 2, 0, l)
        hi = np.clip(t + w // 2, 0, l)
        bands.append((s >= lo) & (s < hi))
        invs.append(np.repeat(1.0 / (hi - lo).astype(np.float32), gw, axis=1))
    return (jnp.asarray(np.stack(bands), dtype=BF16),
            jnp.asarray(np.concatenate(invs, axis=1), dtype=F32))


def kernel(x_prompt, x_sample, cache_a_k, cache_a_v, cache_c_k, cache_c_v, c, c_ctx, norm_w, ada_w, ada_b, even_w_in, even_q_norm_w, even_k_norm_w, even_lam_q1, even_lam_k1, even_lam_q2, even_lam_k2, even_subln_w, even_pool_w, even_pool_scale, even_w_out, gqa_w_in, gqa_q_norm_w, gqa_k_norm_w, gqa_w_out):
    depth, d = norm_w.shape
    n_lat = x_sample.shape[1]
    n_dec = x_sample.shape[0]
    w_b = d - W_A
    gw = w_b // N_POOL

    rows = -(-(1 + n_dec) // 8) * 8
    cond = jnp.concatenate([c_ctx[None], c, jnp.zeros((rows - 1 - n_dec, d), F32)], axis=0)
    mod = _ada_rows(cond, ada_w, ada_b).reshape(depth, rows, 3, d)

    rope_a = _rope_tables(n_lat, DK_A)
    rope_c = _rope_tables(n_lat, HD_C)
    pool_p = _pool_tables(x_prompt.shape[1], gw)
    pool_s = _pool_tables(n_lat, gw)
    hmask = jnp.asarray(np.arange(LANES)[None, :] // DK_A == np.arange(2)[:, None], dtype=BF16)

    y_p, y_s = x_prompt, x_sample
    new_a_k, new_a_v, new_c_k, new_c_v = [], [], [], []
    for i in range(depth):
        j = i // 2
        mod_p = (mod, i, 0, False)
        mod_s = (mod, i, 1, True)
        nw = norm_w[i].reshape(1, d)
        if i % 2 == 0:
            lam_init = 0.8 - 0.6 * math.exp(-0.3 * i)
            w_in = even_w_in[j].astype(BF16)
            w_out = even_w_out[j].astype(BF16)
            poolw = even_pool_w[j].astype(BF16)
            qw = jnp.tile(even_q_norm_w[j], LANES // DK_A).reshape(1, LANES) * (DK_A ** -0.5 * LOG2_E)
            kw = jnp.tile(even_k_norm_w[j], LANES // DK_A).reshape(1, LANES)
            lamv = jnp.stack([even_lam_q1[j], even_lam_k1[j], even_lam_q2[j], even_lam_k2[j]])
            subln = even_subln_w[j].reshape(1, DV_A)
            pscale = even_pool_scale[j].reshape(1, w_b)
            shared = (lamv, hmask, subln, poolw, pscale, w_out, lam_init)

            q, k, v, u, g = _even_in(y_p, mod_p, nw, w_in, qw, kw, None, F32)
            new_a_k.append(k)
            new_a_v.append(v)
            y_p = _even_mix(q, k, v, None, u, g, y_p, mod_p, *pool_p, *shared)
            q, k, v, u, g = _even_in(y_s, mod_s, nw, w_in, qw, kw, rope_a, BF16)
            y_s = _even_mix(q, k, v, (cache_a_k, cache_a_v, j), u, g, y_s, mod_s, *pool_s, *shared)
        else:
            w_in = gqa_w_in[j].astype(BF16)
            w_out = gqa_w_out[j].astype(BF16)
            qw = gqa_q_norm_w[j].reshape(1, HD_C) * (HD_C ** -0.5 * LOG2_E)
            kw = gqa_k_norm_w[j].reshape(1, HD_C)

            q, k, v, g = _odd_in(y_p, mod_p, nw, w_in, qw, kw, None, F32)
            new_c_k.append(k)
            new_c_v.append(v)
            y_p = _odd_mix(q, k, v, None, g, y_p, mod_p, w_out)
            q, k, v, g = _odd_in(y_s, mod_s, nw, w_in, qw, kw, rope_c, BF16)
            y_s = _odd_mix(q, k, v, (cache_c_k, cache_c_v, j), g, y_s, mod_s, w_out)

    return (y_p, y_s, jnp.stack(new_a_k, axis=1), jnp.stack(new_a_v, axis=1),
            jnp.stack(new_c_k, axis=1), jnp.stack(new_c_v, axis=1))
```
